```python
import jax, jax.numpy as jnp
from jax import lax
import numpy as np

D_MODEL = 4096
BATCH = 4
SEQ = 4096
DEPTH = 1

CHUNK = 64
Q_BLOCK = 128
RMS_EPS = 1e-6
L2_EPS = 1e-6

MLA_HEADS = 16
QK_NOPE = 128
QK_ROPE = 64
QK_HEAD = QK_NOPE + QK_ROPE
V_HEAD = 128
Q_LORA = 1024
KV_LORA = 512
ROPE_THETA = 10000.0
MLA_WIDTH = MLA_HEADS * V_HEAD

GDN_HEADS = 16
GDN_DK = 128
GDN_DV = 128
GDN_KEY_WIDTH = GDN_HEADS * GDN_DK
GDN_WIDTH = GDN_HEADS * GDN_DV
GDN_QKV = 2 * GDN_KEY_WIDTH + GDN_WIDTH
CONV_K = 4

N_BRANCH = 2

SPLITS = (Q_LORA, KV_LORA, QK_ROPE, MLA_WIDTH, GDN_QKV, GDN_HEADS, GDN_HEADS, GDN_WIDTH, D_MODEL, D_MODEL)
IN_WIDTH = Q_LORA + KV_LORA + QK_ROPE + MLA_WIDTH + GDN_QKV + 2 * GDN_HEADS + GDN_WIDTH + N_BRANCH * D_MODEL

kernel_name = 'hybrid_mla_gated_deltanet_parallel_block'


def _split_points(sizes):
    pts, acc = [], 0
    for s in sizes[:-1]:
        acc += s
        pts.append(acc)
    return pts


def rmsnorm(x, g):
    xf = x.astype(jnp.float32)
    y = xf * lax.rsqrt(jnp.mean(xf * xf, axis=-1, keepdims=True) + RMS_EPS)
    return (y * g.astype(jnp.float32)).astype(x.dtype)


def l2norm(x):
    xf = x.astype(jnp.float32)
    return xf * lax.rsqrt(jnp.sum(xf * xf, axis=-1, keepdims=True) + L2_EPS)


def rope(x, cos, sin):
    x1, x2 = jnp.split(x, 2, axis=-1)
    return jnp.concatenate([x1 * cos - x2 * sin, x2 * cos + x1 * sin], axis=-1)


def chunk_causal_attention(q, k, v):
    B, S, H, Dq = q.shape
    nb = S // Q_BLOCK
    qb = q.reshape(B, nb, Q_BLOCK, H, Dq).transpose(1, 0, 2, 3, 4)
    key_chunk = jnp.arange(S) // CHUNK
    scale = QK_HEAD ** -0.5

    def one_block(args):
        q_blk, blk = args
        q_chunk = (blk * Q_BLOCK + jnp.arange(Q_BLOCK)) // CHUNK
        s = jnp.einsum('bqhd,bkhd->bhqk', q_blk, k, preferred_element_type=jnp.float32) * scale
        s = jnp.where(key_chunk[None, :] <= q_chunk[:, None], s, -jnp.inf)
        p = jax.nn.softmax(s, axis=-1).astype(v.dtype)
        return jnp.einsum('bhqk,bkhd->bqhd', p, v)

    out = lax.map(one_block, (qb, jnp.arange(nb)))
    return out.transpose(1, 0, 2, 3, 4).reshape(B, S, H, v.shape[-1])


def causal_depthwise_conv(x, w):
    C = x.shape[-1]
    return lax.conv_general_dilated(x, w[:, None, :].astype(x.dtype), window_strides=(1,),
                                    padding=[(CONV_K - 1, 0)], dimension_numbers=('NWC', 'WIO', 'NWC'),
                                    feature_group_count=C)


def gated_delta_rule(q, k, v, g, beta):
    B, S, H, Dk = q.shape
    Dv = v.shape[-1]
    nc = S // CHUNK
    f32 = jnp.float32

    def to_chunks(t):
        t = t.astype(f32)
        return t.reshape(B, nc, CHUNK, H, *t.shape[3:]).swapaxes(2, 3)

    q = to_chunks(q) * (Dk ** -0.5)
    k, v, g, beta = to_chunks(k), to_chunks(v), to_chunks(g), to_chunks(beta)
    gc = jnp.cumsum(g, axis=-1)
    causal = jnp.tril(jnp.ones((CHUNK, CHUNK), dtype=bool))
    strict = jnp.tril(jnp.ones((CHUNK, CHUNK), dtype=bool), -1)
    diff = gc[..., :, None] - gc[..., None, :]
    decay = jnp.where(causal, jnp.exp(jnp.where(causal, diff, 0.0)), 0.0)
    k_beta = k * beta[..., None]
    v_beta = v * beta[..., None]
    low = jnp.where(strict, jnp.einsum('bnhid,bnhjd->bnhij', k_beta, k) * decay, 0.0)
    a = low + jnp.eye(CHUNK, dtype=f32)
    rhs = jnp.concatenate([v_beta, k_beta * jnp.exp(gc)[..., None]], axis=-1)
    sol = lax.linalg.triangular_solve(a, rhs, left_side=True, lower=True)
    u, w = sol[..., :Dv], sol[..., Dv:]
    attn_intra = jnp.einsum('bnhid,bnhjd->bnhij', q, k) * decay
    q_dec = q * jnp.exp(gc)[..., None]
    k_dec = k * jnp.exp(gc[..., -1:] - gc)[..., None]
    g_last = jnp.exp(gc[..., -1])

    def step(state, inp):
        u_c, w_c, a_c, qd_c, kd_c, gl_c = inp
        v_new = u_c - jnp.einsum('bhcd,bhde->bhce', w_c, state)
        o = jnp.einsum('bhcd,bhde->bhce', qd_c, state) + jnp.einsum('bhij,bhje->bhie', a_c, v_new)
        state = state * gl_c[..., None, None] + jnp.einsum('bhcd,bhce->bhde', kd_c, v_new)
        return state, o

    xs = tuple(jnp.moveaxis(t, 1, 0) for t in (u, w, attn_intra, q_dec, k_dec, g_last))
    s0 = jnp.zeros((B, H, Dk, Dv), f32)
    _, o = lax.scan(step, s0, xs)
    return o.transpose(1, 0, 3, 2, 4).reshape(B, S, H, Dv)


def setup_inputs(seed: int = 0) -> dict:
    key = jax.random.key(seed)
    ks = jax.random.split(key, 20)
    f32 = jnp.float32
    nrm = lambda k, shape, fan_in: jax.random.normal(k, shape, f32) * (fan_in ** -0.5)
    x = jax.random.normal(ks[0], (BATCH, SEQ, D_MODEL), f32)
    offset = jax.random.randint(ks[1], (BATCH, 1), 0, 16) * CHUNK
    positions = (offset + jnp.arange(SEQ, dtype=jnp.int32)[None, :]).astype(jnp.int32)
    gain = lambda k, n: 1.0 + 0.02 * jax.random.normal(k, (n,), f32)
    dt = jnp.exp(jax.random.uniform(ks[14], (GDN_HEADS,), f32, np.log(1e-3), np.log(1e-1)))
    return {
        'x': x,
        'positions': positions,
        'w_norm': gain(ks[2], D_MODEL),
        'w_in': nrm(ks[3], (D_MODEL, IN_WIDTH), D_MODEL),
        'b_gate': 0.02 * jax.random.normal(ks[4], (N_BRANCH, D_MODEL), f32),
        'w_q_up': nrm(ks[5], (Q_LORA, MLA_HEADS * QK_HEAD), Q_LORA),
        'w_kv_up': nrm(ks[6], (KV_LORA, MLA_HEADS * (QK_NOPE + V_HEAD)), KV_LORA),
        'g_q_latent': gain(ks[7], Q_LORA),
        'g_kv_latent': gain(ks[8], KV_LORA),
        'g_q_head': gain(ks[9], QK_HEAD),
        'g_k_head': gain(ks[10], QK_HEAD),
        'w_o_mla': nrm(ks[11], (MLA_WIDTH, D_MODEL), MLA_WIDTH),
        'conv_w': nrm(ks[12], (CONV_K, GDN_QKV), CONV_K),
        'a_log': jnp.log(jax.random.uniform(ks[13], (GDN_HEADS,), f32, 1.0, 16.0)),
        'dt_bias': dt + jnp.log(-jnp.expm1(-dt)),
        'g_gdn_out': gain(ks[15], GDN_DV),
        'w_o_gdn': nrm(ks[16], (GDN_WIDTH, D_MODEL), GDN_WIDTH),
        'w_out': nrm(ks[17], (D_MODEL, D_MODEL), D_MODEL),
    }


def reference(x, positions, w_norm, w_in, b_gate, w_q_up, w_kv_up, g_q_latent, g_kv_latent, g_q_head, g_k_head,
              w_o_mla, conv_w, a_log, dt_bias, g_gdn_out, w_o_gdn, w_out):
    B, S, _ = x.shape
    f32 = jnp.float32
    for _layer in range(DEPTH):
        xn = rmsnorm(x, w_norm)
        proj = jnp.einsum('bsd,de->bse', xn, w_in)
        (c_q, c_kv, k_rope, z_mla, qkv_g, b_logit, a_logit, z_gdn, gate_mla, gate_gdn) = jnp.split(
            proj, _split_points(SPLITS), axis=-1)

        q = jnp.einsum('bsr,re->bse', rmsnorm(c_q, g_q_latent), w_q_up).reshape(B, S, MLA_HEADS, QK_HEAD)
        kv = jnp.einsum('bsr,re->bse', rmsnorm(c_kv, g_kv_latent), w_kv_up).reshape(B, S, MLA_HEADS, QK_NOPE + V_HEAD)
        k_nope, v = kv[..., :QK_NOPE], kv[..., QK_NOPE:]
        k = jnp.concatenate([k_nope, jnp.broadcast_to(k_rope[:, :, None, :], (B, S, MLA_HEADS, QK_ROPE))], axis=-1)
        q = rmsnorm(q, g_q_head)
        k = rmsnorm(k, g_k_head)
        inv_freq = ROPE_THETA ** (-jnp.arange(0, QK_ROPE, 2, dtype=f32) / QK_ROPE)
        ang = positions.astype(f32)[..., None] * inv_freq
        cos = jnp.cos(ang)[:, :, None, :].astype(x.dtype)
        sin = jnp.sin(ang)[:, :, None, :].astype(x.dtype)
        q = jnp.concatenate([q[..., :QK_NOPE], rope(q[..., QK_NOPE:], cos, sin)], axis=-1)
        k = jnp.concatenate([k[..., :QK_NOPE], rope(k[..., QK_NOPE:], cos, sin)], axis=-1)
        o_mla = chunk_causal_attention(q, k, v).reshape(B, S, MLA_WIDTH)
        y_mla = jnp.einsum('bse,ed->bsd', o_mla * jax.nn.silu(z_mla), w_o_mla)

        qkv = jax.nn.silu(causal_depthwise_conv(qkv_g, conv_w))
        qg, kg, vg = jnp.split(qkv, [GDN_KEY_WIDTH, 2 * GDN_KEY_WIDTH], axis=-1)
        qg = l2norm(qg.reshape(B, S, GDN_HEADS, GDN_DK))
        kg = l2norm(kg.reshape(B, S, GDN_HEADS, GDN_DK))
        vg = vg.reshape(B, S, GDN_HEADS, GDN_DV)
        beta = jax.nn.sigmoid(b_logit.astype(f32))
        g_log = -jnp.exp(a_log.astype(f32)) * jax.nn.softplus(a_logit.astype(f32) + dt_bias.astype(f32))
        o_gdn = gated_delta_rule(qg, kg, vg, g_log, beta).astype(x.dtype)
        o_gdn = rmsnorm(o_gdn, g_gdn_out).reshape(B, S, GDN_WIDTH) * jax.nn.silu(z_gdn)
        y_gdn = jnp.einsum('bse,ed->bsd', o_gdn, w_o_gdn)

        h = jax.nn.sigmoid(gate_mla + b_gate[0]) * y_mla + jax.nn.sigmoid(gate_gdn + b_gate[1]) * y_gdn
        x = x + jnp.einsum('bsd,de->bse', h, w_out)
    return x
```

```python
import functools

import jax
import jax.numpy as jnp
from jax import lax
from jax.experimental import pallas as pl
from jax.experimental.pallas import tpu as pltpu

F32 = jnp.float32
BF16 = jnp.bfloat16

D_MODEL = 4096
CHUNK = 64
RMS_EPS = 1e-6
L2_EPS = 1e-6

MLA_HEADS = 16
QK_NOPE = 128
QK_ROPE = 64
QK_HEAD = QK_NOPE + QK_ROPE
V_HEAD = 128
Q_LORA = 1024
KV_LORA = 512
ROPE_THETA = 10000.0
MLA_WIDTH = MLA_HEADS * V_HEAD

GDN_HEADS = 16
GDN_DK = 128
GDN_DV = 128
GDN_KEY_WIDTH = GDN_HEADS * GDN_DK
GDN_WIDTH = GDN_HEADS * GDN_DV
GDN_QKV = 2 * GDN_KEY_WIDTH + GDN_WIDTH
CONV_K = 4

LANES = 128
QK_PAD = 256
GDN_BLOCK = 128
GDN_GROUP = 4

OFF_GATE_MLA = 0
OFF_GATE_GDN = OFF_GATE_MLA + D_MODEL
OFF_QKV = OFF_GATE_GDN + D_MODEL
OFF_Z_MLA = OFF_QKV + GDN_QKV
OFF_Z_GDN = OFF_Z_MLA + MLA_WIDTH
OFF_CQ = OFF_Z_GDN + GDN_WIDTH
OFF_CKV = OFF_CQ + Q_LORA
OFF_KROPE = OFF_CKV + KV_LORA
PROJ_WIDTH = 20480

VMEM_LIMIT = 56 * 1024 * 1024


def _cparams(sem):
    return pltpu.CompilerParams(dimension_semantics=sem, vmem_limit_bytes=VMEM_LIMIT)


def _sigmoid(x):
    return 1.0 / (1.0 + jnp.exp(-x))


def _norm_gate_kernel(x_ref, wn_ref, wbd_ref, par_ref, xn_ref, gb_ref):
    x = x_ref[...]
    ms = jnp.mean(x * x, axis=-1, keepdims=True)
    xn = (x * lax.rsqrt(ms + RMS_EPS) * wn_ref[...]).astype(BF16)
    xn_ref[...] = xn
    bd = jnp.dot(xn, wbd_ref[...], preferred_element_type=F32)
    neg_a = -jnp.exp(par_ref[0:1, :])
    z = bd + par_ref[1:2, :]
    softplus = jnp.maximum(z, 0.0) + jnp.log1p(jnp.exp(-jnp.abs(z)))
    lane = lax.broadcasted_iota(jnp.int32, bd.shape, 1) % LANES
    gb_ref[...] = jnp.where(lane < 16, _sigmoid(bd), neg_a * softplus)


def _norm_gate(x2, w_norm, w_bd, par, tm=256):
    n = x2.shape[0]
    gw = w_bd.shape[1]
    return pl.pallas_call(
        _norm_gate_kernel,
        grid=(n // tm,),
        in_specs=[
            pl.BlockSpec((tm, D_MODEL), lambda i: (i, 0)),
            pl.BlockSpec((1, D_MODEL), lambda i: (0, 0)),
            pl.BlockSpec((D_MODEL, gw), lambda i: (0, 0)),
            pl.BlockSpec((8, gw), lambda i: (0, 0)),
        ],
        out_specs=[
            pl.BlockSpec((tm, D_MODEL), lambda i: (i, 0)),
            pl.BlockSpec((tm, gw), lambda i: (i, 0)),
        ],
        out_shape=[
            jax.ShapeDtypeStruct((n, D_MODEL), BF16),
            jax.ShapeDtypeStruct((n, gw), F32),
        ],
        compiler_params=_cparams(("parallel",)),
        name="norm_gate",
    )(x2, w_norm, w_bd, par)


def _matmul_kernel(a_ref, b_ref, o_ref):
    o_ref[...] = jnp.dot(a_ref[...], b_ref[...], preferred_element_type=F32).astype(o_ref.dtype)


def _matmul(a, b, out_dtype, tm, tn, name):
    m, k = a.shape
    n = b.shape[1]
    return pl.pallas_call(
        _matmul_kernel,
        grid=(m // tm, n // tn),
        in_specs=[
            pl.BlockSpec((tm, k), lambda i, j: (i, 0)),
            pl.BlockSpec((k, tn), lambda i, j: (0, j)),
        ],
        out_specs=pl.BlockSpec((tm, tn), lambda i, j: (i, j)),
        out_shape=jax.ShapeDtypeStruct((m, n), out_dtype),
        compiler_params=_cparams(("parallel", "parallel")),
        name=name,
    )(a, b)


def _rope_tables(pos_ref, invf_ref):
    ang = pos_ref[...].astype(F32) * invf_ref[...]
    cos = jnp.cos(ang)
    sin = jnp.sin(ang)
    lane = lax.broadcasted_iota(jnp.int32, ang.shape, 1)
    c = jnp.where(lane < QK_ROPE, cos, 0.0)
    s_lo = jnp.where(lane < QK_ROPE // 2, -sin, 0.0)
    s_hi = jnp.where((lane >= QK_ROPE // 2) & (lane < QK_ROPE), sin, 0.0)
    return c, s_lo, s_hi


def _rope(r, tables):
    c, s_lo, s_hi = tables
    half = QK_ROPE // 2
    return r * c + pltpu.roll(r, LANES - half, 1) * s_lo + pltpu.roll(r, half, 1) * s_hi


def _rms_rows(c, gain):
    ms = jnp.mean(c * c, axis=-1, keepdims=True)
    return c * lax.rsqrt(ms + RMS_EPS) * gain


def _q_prep_kernel(cq_ref, pos_ref, invf_ref, gl_ref, w_ref, gh_ref, q_ref):
    cn = _rms_rows(cq_ref[...].astype(F32), gl_ref[...]).astype(BF16)
    q = jnp.dot(cn, w_ref[...], preferred_element_type=F32)
    tables = _rope_tables(pos_ref, invf_ref)
    g_nope = gh_ref[:, :QK_NOPE]
    g_rope = gh_ref[:, QK_NOPE:]
    scale = QK_HEAD ** -0.5
    for h in range(MLA_HEADS):
        a = q[:, h * QK_PAD:h * QK_PAD + QK_NOPE]
        r = q[:, h * QK_PAD + QK_NOPE:(h + 1) * QK_PAD]
        ss = jnp.sum(a * a + r * r, axis=-1, keepdims=True)
        inv = lax.rsqrt(ss * (1.0 / QK_HEAD) + RMS_EPS) * scale
        q_ref[:, h * QK_PAD:h * QK_PAD + QK_NOPE] = (a * inv * g_nope).astype(BF16)
        q_ref[:, h * QK_PAD + QK_NOPE:(h + 1) * QK_PAD] = (_rope(r * g_rope, tables) * inv).astype(BF16)


def _q_prep(proj, pos2, invf, g_lat, w_q, g_head, tm=256):
    n = proj.shape[0]
    width = MLA_HEADS * QK_PAD
    return pl.pallas_call(
        _q_prep_kernel,
        grid=(n // tm,),
        in_specs=[
            pl.BlockSpec((tm, Q_LORA), lambda i: (i, OFF_CQ // Q_LORA)),
            pl.BlockSpec((tm, 1), lambda i: (i, 0)),
            pl.BlockSpec((1, LANES), lambda i: (0, 0)),
            pl.BlockSpec((1, Q_LORA), lambda i: (0, 0)),
            pl.BlockSpec((Q_LORA, width), lambda i: (0, 0)),
            pl.BlockSpec((1, QK_PAD), lambda i: (0, 0)),
        ],
        out_specs=pl.BlockSpec((tm, width), lambda i: (i, 0)),
        out_shape=jax.ShapeDtypeStruct((n, width), BF16),
        compiler_params=_cparams(("parallel",)),
        name="q_prep",
    )(proj, pos2, invf, g_lat, w_q, g_head)


def _kv_prep_kernel(ckv_ref, kr_ref, pos_ref, invf_ref, gl_ref, w_ref, gkn_ref, gkr_ref, k_ref, v_ref):
    cn = _rms_rows(ckv_ref[...].astype(F32), gl_ref[...]).astype(BF16)
    kv = jnp.dot(cn, w_ref[...], preferred_element_type=F32)
    tables = _rope_tables(pos_ref, invf_ref)
    kr = kr_ref[...].astype(F32)
    ss_r = jnp.sum(kr * kr, axis=-1, keepdims=True)
    rot = _rope(kr * gkr_ref[...], tables)
    g_nope = gkn_ref[...]
    for h in range(MLA_HEADS):
        kn = kv[:, h * QK_PAD:h * QK_PAD + QK_NOPE]
        v = kv[:, h * QK_PAD + QK_NOPE:(h + 1) * QK_PAD]
        ss = jnp.sum(kn * kn, axis=-1, keepdims=True) + ss_r
        inv = lax.rsqrt(ss * (1.0 / QK_HEAD) + RMS_EPS)
        k_ref[:, h * QK_PAD:h * QK_PAD + QK_NOPE] = (kn * inv * g_nope).astype(BF16)
        k_ref[:, h * QK_PAD + QK_NOPE:(h + 1) * QK_PAD] = (rot * inv).astype(BF16)
        v_ref[:, h * V_HEAD:(h + 1) * V_HEAD] = v.astype(BF16)


def _kv_prep(proj, pos2, invf, g_lat, w_kv, g_kn, g_kr, tm=256):
    n = proj.shape[0]
    kwidth = MLA_HEADS * QK_PAD
    return pl.pallas_call(
        _kv_prep_kernel,
        grid=(n // tm,),
        in_specs=[
            pl.BlockSpec((tm, KV_LORA), lambda i: (i, OFF_CKV // KV_LORA)),
            pl.BlockSpec((tm, LANES), lambda i: (i, OFF_KROPE // LANES)),
            pl.BlockSpec((tm, 1), lambda i: (i, 0)),
            pl.BlockSpec((1, LANES), lambda i: (0, 0)),
            pl.BlockSpec((1, KV_LORA), lambda i: (0, 0)),
            pl.BlockSpec((KV_LORA, kwidth), lambda i: (0, 0)),
            pl.BlockSpec((1, LANES), lambda i: (0, 0)),
            pl.BlockSpec((1, LANES), lambda i: (0, 0)),
        ],
        out_specs=[
            pl.BlockSpec((tm, kwidth), lambda i: (i, 0)),
            pl.BlockSpec((tm, MLA_WIDTH), lambda i: (i, 0)),
        ],
        out_shape=[
            jax.ShapeDtypeStruct((n, kwidth), BF16),
            jax.ShapeDtypeStruct((n, MLA_WIDTH), BF16),
        ],
        compiler_params=_cparams(("parallel",)),
        name="kv_prep",
    )(proj, proj, pos2, invf, g_lat, w_kv, g_kn, g_kr)


def _attn_kernel(q_ref, k_ref, v_ref, z_ref, o_ref, *, tq):
    i = pl.program_id(2)
    q = q_ref[...]

    def step(k, v, carry, mask):
        m, l, acc = carry
        s = lax.dot_general(q, k, (((1,), (1,)), ((), ())), preferred_element_type=F32)
        if mask is not None:
            s = jnp.where(mask, s, -jnp.inf)
        m_new = jnp.maximum(m, jnp.max(s, axis=-1, keepdims=True))
        alpha = jnp.exp(m - m_new)
        p = jnp.exp(s - m_new)
        l = alpha * l + jnp.sum(p, axis=-1, keepdims=True)
        acc = alpha * acc + jnp.dot(p.astype(BF16), v, preferred_element_type=F32)
        return m_new, l, acc

    def body(j, carry):
        off = pl.multiple_of(j * tq, tq)
        return step(k_ref[pl.ds(off, tq), :], v_ref[pl.ds(off, tq), :], carry, None)

    init = (jnp.full((tq, 1), -1e30, F32), jnp.zeros((tq, 1), F32), jnp.zeros((tq, V_HEAD), F32))
    carry = lax.fori_loop(0, i, body, init)
    row = lax.broadcasted_iota(jnp.int32, (tq, tq), 0) // CHUNK
    col = lax.broadcasted_iota(jnp.int32, (tq, tq), 1) // CHUNK
    off = pl.multiple_of(i * tq, tq)
    _, l, acc = step(k_ref[pl.ds(off, tq), :], v_ref[pl.ds(off, tq), :], carry, col <= row)
    z = z_ref[...].astype(F32)
    o_ref[...] = (acc / l * (z * _sigmoid(z))).astype(o_ref.dtype)


def _attention(q, k, v, proj, batch, seq, tq=256):
    n = q.shape[0]
    nq = seq // tq
    return pl.pallas_call(
        functools.partial(_attn_kernel, tq=tq),
        grid=(batch, MLA_HEADS, nq),
        in_specs=[
            pl.BlockSpec((tq, QK_PAD), lambda b, h, i: (b * nq + i, h)),
            pl.BlockSpec((seq, QK_PAD), lambda b, h, i: (b, h)),
            pl.BlockSpec((seq, V_HEAD), lambda b, h, i: (b, h)),
            pl.BlockSpec((tq, V_HEAD), lambda b, h, i: (b * nq + i, OFF_Z_MLA // V_HEAD + h)),
        ],
        out_specs=pl.BlockSpec((tq, V_HEAD), lambda b, h, i: (b * nq + i, h)),
        out_shape=jax.ShapeDtypeStruct((n, MLA_WIDTH), BF16),
        compiler_params=_cparams(("parallel", "parallel", "arbitrary")),
        name="mla_attention",
    )(q, k, v, proj)


def _gdn_prep_kernel(q_ref, k_ref, v_ref, qh_ref, kh_ref, vh_ref, cwq_ref, cwk_ref, cwv_ref, gb_ref,
                     u_ref, w_ref, qd_ref, ai_ref, kdt_ref, gl_ref, *, tt):
    first = pl.program_id(1) == 0
    nb = tt // GDN_BLOCK
    blk = GDN_BLOCK

    gb = gb_ref[...]
    rmod = lax.broadcasted_iota(jnp.int32, (tt, LANES), 0) % blk
    gc = gb
    s = 1
    while s < blk:
        gc = gc + jnp.where(rmod >= s, pltpu.roll(gc, s, 0), 0.0)
        s *= 2

    ri = lax.broadcasted_iota(jnp.int32, (blk, blk), 0)
    ci = lax.broadcasted_iota(jnp.int32, (blk, blk), 1)
    causal = ri >= ci
    strict = ri > ci
    gc_t = [gc[p * blk:(p + 1) * blk].T for p in range(nb)]

    def conv_silu(x_ref, h_ref, cw_ref, c):
        x = x_ref[:, c].astype(F32)
        halo = h_ref[:, c].astype(F32)[8:16]
        halo = jnp.where(first, 0.0, halo)
        xe = jnp.concatenate([halo, x], axis=0)
        cw = cw_ref[:, c]
        y = xe * cw[CONV_K - 1:CONV_K]
        for sh in range(1, CONV_K):
            y = y + pltpu.roll(xe, sh, 0) * cw[CONV_K - 1 - sh:CONV_K - sh]
        y = y[8:]
        return y * _sigmoid(y)

    for hh in range(GDN_GROUP):
        c = slice(hh * LANES, (hh + 1) * LANES)
        q = conv_silu(q_ref, qh_ref, cwq_ref, c)
        k = conv_silu(k_ref, kh_ref, cwk_ref, c)
        v = conv_silu(v_ref, vh_ref, cwv_ref, c)
        q = q * (lax.rsqrt(jnp.sum(q * q, axis=-1, keepdims=True) + L2_EPS) * (GDN_DK ** -0.5))
        k = k * lax.rsqrt(jnp.sum(k * k, axis=-1, keepdims=True) + L2_EPS)
        beta_b = jnp.broadcast_to(gb[:, hh:hh + 1], (tt, LANES))
        gcol_b = jnp.broadcast_to(gc[:, 16 + hh:17 + hh], (tt, LANES))
        for p in range(nb):
            r = slice(p * blk, (p + 1) * blk)
            qp, kp, vp = q[r], k[r], v[r]
            gcol = gcol_b[r]
            grow = gc_t[p][16 + hh:17 + hh, :]
            diff = gcol - grow
            decay = jnp.where(causal, jnp.exp(jnp.where(causal, diff, 0.0)), 0.0)
            kb = kp * beta_b[r]
            kbf = kp.astype(BF16)
            lhs = jnp.concatenate([kb.astype(BF16), qp.astype(BF16)], axis=0)
            kq = lax.dot_general(lhs, kbf, (((1,), (1,)), ((), ())), preferred_element_type=F32)
            neg_l = jnp.where(strict, -(kq[:blk] * decay), 0.0)
            ai_ref[r, c] = (kq[blk:] * decay).astype(BF16)
            mb = neg_l.astype(BF16)
            e = neg_l
            pw = jnp.dot(mb, mb, preferred_element_type=F32)
            span = 2
            while span * 2 < blk:
                pwb = pw.astype(BF16)
                both = jnp.dot(jnp.concatenate([e.astype(BF16), pwb], axis=0), pwb, preferred_element_type=F32)
                e = e + pw + both[:blk]
                pw = both[blk:]
                span *= 2
            e = e + pw + jnp.dot(e.astype(BF16), pw.astype(BF16), preferred_element_type=F32)
            egc = jnp.exp(gcol)
            vb = vp * beta_b[r]
            kbe = kb * egc
            rhs = jnp.concatenate([vb.astype(BF16), kbe.astype(BF16)], axis=1)
            sol = jnp.dot(e.astype(BF16), rhs, preferred_element_type=F32)
            u_ref[r, c] = vb + sol[:, :LANES]
            w_ref[r, c] = (kbe + sol[:, LANES:]).astype(BF16)
            qd_ref[r, c] = (qp * egc).astype(BF16)
            g_last = jnp.broadcast_to(gcol[blk - 1:blk, :], (blk, LANES))
            kd = kp * jnp.exp(g_last - gcol)
            kdt_ref[c, r] = kd.T.astype(BF16)
            gl_ref[p * 8:(p + 1) * 8, c] = jnp.exp(g_last[:8])


def _gdn_prep(proj, conv_w, gb, batch, seq, tt=256):
    n = proj.shape[0]
    gw = GDN_GROUP * LANES
    ns = seq // tt
    ngrp = GDN_HEADS // GDN_GROUP
    col = lambda base: (lambda b, i, g: (b * ns + i, base // gw + g))
    halo = lambda base: (lambda b, i, g: (jnp.maximum((b * ns + i) * (tt // 16) - 1, 0), base // gw + g))
    cwc = lambda base: (lambda b, i, g: (0, base // gw + g))
    tok = lambda b, i, g: (b * ns + i, g)
    return pl.pallas_call(
        functools.partial(_gdn_prep_kernel, tt=tt),
        grid=(batch, ns, ngrp),
        in_specs=[
            pl.BlockSpec((tt, gw), col(OFF_QKV)),
            pl.BlockSpec((tt, gw), col(OFF_QKV + GDN_KEY_WIDTH)),
            pl.BlockSpec((tt, gw), col(OFF_QKV + 2 * GDN_KEY_WIDTH)),
            pl.BlockSpec((16, gw), halo(OFF_QKV)),
            pl.BlockSpec((16, gw), halo(OFF_QKV + GDN_KEY_WIDTH)),
            pl.BlockSpec((16, gw), halo(OFF_QKV + 2 * GDN_KEY_WIDTH)),
            pl.BlockSpec((CONV_K, gw), cwc(0)),
            pl.BlockSpec((CONV_K, gw), cwc(GDN_KEY_WIDTH)),
            pl.BlockSpec((CONV_K, gw), cwc(2 * GDN_KEY_WIDTH)),
            pl.BlockSpec((tt, LANES), tok),
        ],
        out_specs=[
            pl.BlockSpec((tt, gw), tok),
            pl.BlockSpec((tt, gw), tok),
            pl.BlockSpec((tt, gw), tok),
            pl.BlockSpec((tt, gw), tok),
            pl.BlockSpec((gw, tt), lambda b, i, g: (b * ngrp + g, i)),
            pl.BlockSpec((tt // GDN_BLOCK * 8, gw), tok),
        ],
        out_shape=[
            jax.ShapeDtypeStruct((n, GDN_WIDTH), F32),
            jax.ShapeDtypeStruct((n, GDN_WIDTH), BF16),
            jax.ShapeDtypeStruct((n, GDN_WIDTH), BF16),
            jax.ShapeDtypeStruct((n, GDN_WIDTH), BF16),
            jax.ShapeDtypeStruct((batch * GDN_HEADS * GDN_DK, seq), BF16),
            jax.ShapeDtypeStruct((n // GDN_BLOCK * 8, GDN_WIDTH), F32),
        ],
        compiler_params=_cparams(("parallel", "parallel", "parallel")),
        name="gdn_prep",
    )(proj, proj, proj, proj, proj, proj, conv_w, conv_w, conv_w, gb)


def _gdn_scan_kernel(u_ref, w_ref, qd_ref, ai_ref, kdt_ref, gl_ref, z_ref, g_ref, o_ref, s_ref, *, ts, hg):
    @pl.when(pl.program_id(2) == 0)
    def _():
        s_ref[...] = jnp.zeros_like(s_ref)

    blk = GDN_BLOCK
    gain = g_ref[...]
    for p in range(ts // blk):
        r = slice(p * blk, (p + 1) * blk)
        for hh in range(hg):
            c = slice(hh * LANES, (hh + 1) * LANES)
            state = s_ref[hh]
            lhs1 = jnp.concatenate([w_ref[r, c], qd_ref[r, c]], axis=0)
            r1 = jnp.dot(lhs1, state.astype(BF16), preferred_element_type=F32)
            v_new = u_ref[r, c] - r1[:blk]
            lhs2 = jnp.concatenate([ai_ref[r, c], kdt_ref[c, r]], axis=0)
            r2 = jnp.dot(lhs2, v_new.astype(BF16), preferred_element_type=F32)
            o = r1[blk:] + r2[:blk]
            gl = jnp.concatenate([gl_ref[p * 8:(p + 1) * 8, c]] * (blk // 8), axis=0)
            s_ref[hh] = state * gl + r2[blk:]
            z = z_ref[r, c].astype(F32)
            o_ref[r, c] = (_rms_rows(o, gain) * (z * _sigmoid(z))).astype(o_ref.dtype)


def _gdn_scan(u, w, qd, ai, kdt, gl8, proj, g_out, batch, seq, ts=512, hg=8):
    n = u.shape[0]
    gw = hg * LANES
    ns = seq // ts
    ngrp = GDN_HEADS // hg
    tok = lambda b, g, i: (b * ns + i, g)
    return pl.pallas_call(
        functools.partial(_gdn_scan_kernel, ts=ts, hg=hg),
        grid=(batch, ngrp, ns),
        in_specs=[
            pl.BlockSpec((ts, gw), tok),
            pl.BlockSpec((ts, gw), tok),
            pl.BlockSpec((ts, gw), tok),
            pl.BlockSpec((ts, gw), tok),
            pl.BlockSpec((gw, ts), lambda b, g, i: (b * ngrp + g, i)),
            pl.BlockSpec((ts // GDN_BLOCK * 8, gw), tok),
            pl.BlockSpec((ts, gw), lambda b, g, i: (b * ns + i, OFF_Z_GDN // gw + g)),
            pl.BlockSpec((1, LANES), lambda b, g, i: (0, 0)),
        ],
        out_specs=pl.BlockSpec((ts, gw), tok),
        out_shape=jax.ShapeDtypeStruct((n, GDN_WIDTH), BF16),
        scratch_shapes=[pltpu.VMEM((hg, GDN_DK, GDN_DV), F32)],
        compiler_params=_cparams(("parallel", "parallel", "arbitrary")),
        name="gdn_scan",
    )(u, w, qd, ai, kdt, gl8, proj, g_out)


def _merge_kernel(a1_ref, a2_ref, w1_ref, w2_ref, g1_ref, g2_ref, b_ref, h_ref):
    y1 = jnp.dot(a1_ref[...], w1_ref[...], preferred_element_type=F32)
    y2 = jnp.dot(a2_ref[...], w2_ref[...], preferred_element_type=F32)
    s1 = _sigmoid(g1_ref[...].astype(F32) + b_ref[0:1, :])
    s2 = _sigmoid(g2_ref[...].astype(F32) + b_ref[1:2, :])
    h_ref[...] = (s1 * y1 + s2 * y2).astype(h_ref.dtype)


def _merge(o_mla, o_gdn, w_o_mla, w_o_gdn, proj, b_gate, tm=512, tn=1024):
    n = o_mla.shape[0]
    return pl.pallas_call(
        _merge_kernel,
        grid=(n // tm, D_MODEL // tn),
        in_specs=[
            pl.BlockSpec((tm, MLA_WIDTH), lambda i, j: (i, 0)),
            pl.BlockSpec((tm, GDN_WIDTH), lambda i, j: (i, 0)),
            pl.BlockSpec((MLA_WIDTH, tn), lambda i, j: (0, j)),
            pl.BlockSpec((GDN_WIDTH, tn), lambda i, j: (0, j)),
            pl.BlockSpec((tm, tn), lambda i, j: (i, OFF_GATE_MLA // tn + j)),
            pl.BlockSpec((tm, tn), lambda i, j: (i, OFF_GATE_GDN // tn + j)),
            pl.BlockSpec((2, tn), lambda i, j: (0, j)),
        ],
        out_specs=pl.BlockSpec((tm, tn), lambda i, j: (i, j)),
        out_shape=jax.ShapeDtypeStruct((n, D_MODEL), BF16),
        compiler_params=_cparams(("parallel", "parallel")),
        name="merge",
    )(o_mla, o_gdn, w_o_mla, w_o_gdn, proj, proj, b_gate)


def _out_kernel(h_ref, w_ref, x_ref, o_ref):
    o_ref[...] = x_ref[...] + jnp.dot(h_ref[...], w_ref[...], preferred_element_type=F32)


def _out_proj(h, w_out, x2, tm=1024, tn=1024):
    n = h.shape[0]
    return pl.pallas_call(
        _out_kernel,
        grid=(n // tm, D_MODEL // tn),
        in_specs=[
            pl.BlockSpec((tm, D_MODEL), lambda i, j: (i, 0)),
            pl.BlockSpec((D_MODEL, tn), lambda i, j: (0, j)),
            pl.BlockSpec((tm, tn), lambda i, j: (i, j)),
        ],
        out_specs=pl.BlockSpec((tm, tn), lambda i, j: (i, j)),
        out_shape=jax.ShapeDtypeStruct((n, D_MODEL), F32),
        compiler_params=_cparams(("parallel", "parallel")),
        name="out_proj",
    )(h, w_out, x2)


def _pad_cols(a, width):
    return jnp.pad(a, ((0, 0), (0, width - a.shape[1])))


def kernel(x, positions, w_norm, w_in, b_gate, w_q_up, w_kv_up, g_q_latent, g_kv_latent, g_q_head, g_k_head,
           w_o_mla, conv_w, a_log, dt_bias, g_gdn_out, w_o_gdn, w_out):
    batch, seq, _ = x.shape
    n = batch * seq
    x2 = x.reshape(n, D_MODEL)

    o_cq, o_ckv, o_kr = 0, Q_LORA, Q_LORA + KV_LORA
    o_zm = o_kr + QK_ROPE
    o_qkv = o_zm + MLA_WIDTH
    o_b = o_qkv + GDN_QKV
    o_a = o_b + GDN_HEADS
    o_zg = o_a + GDN_HEADS
    o_gm = o_zg + GDN_WIDTH
    o_gg = o_gm + D_MODEL
    seg = lambda lo, width: w_in[:, lo:lo + width]
    w_in_p = jnp.concatenate([
        seg(o_gm, D_MODEL), seg(o_gg, D_MODEL), seg(o_qkv, GDN_QKV), seg(o_zm, MLA_WIDTH), seg(o_zg, GDN_WIDTH),
        seg(o_cq, Q_LORA), seg(o_ckv, KV_LORA), seg(o_kr, QK_ROPE),
        jnp.zeros((D_MODEL, PROJ_WIDTH - OFF_KROPE - QK_ROPE), w_in.dtype)], axis=1).astype(BF16)

    ngrp = GDN_HEADS // GDN_GROUP

    def gate_cols(beta_part, decay_part, fill_rows):
        cols = []
        for g in range(ngrp):
            hs = slice(g * GDN_GROUP, (g + 1) * GDN_GROUP)
            cols += [beta_part[:, hs], jnp.zeros((fill_rows, 16 - GDN_GROUP), F32),
                     decay_part[:, hs], jnp.zeros((fill_rows, LANES - 16 - GDN_GROUP), F32)]
        return jnp.concatenate(cols, axis=1)

    w_bd = gate_cols(seg(o_b, GDN_HEADS), seg(o_a, GDN_HEADS), D_MODEL).astype(BF16)
    zeros_h = jnp.zeros((1, GDN_HEADS), F32)
    par = jnp.concatenate([gate_cols(zeros_h, a_log[None, :], 1), gate_cols(zeros_h, dt_bias[None, :], 1),
                           jnp.zeros((6, ngrp * LANES), F32)], axis=0)

    w_q = jnp.pad(w_q_up.reshape(Q_LORA, MLA_HEADS, QK_HEAD), ((0, 0), (0, 0), (0, QK_PAD - QK_HEAD)))
    w_q = w_q.reshape(Q_LORA, MLA_HEADS * QK_PAD).astype(BF16)
    w_kv = w_kv_up.astype(BF16)
    g_qh = _pad_cols(g_q_head[None, :], QK_PAD)
    g_kn = g_k_head[None, :QK_NOPE]
    g_kr = _pad_cols(g_k_head[None, QK_NOPE:], LANES)
    inv_freq = ROPE_THETA ** (-jnp.arange(0, QK_ROPE, 2, dtype=F32) / QK_ROPE)
    invf = jnp.tile(inv_freq, LANES // (QK_ROPE // 2))[None, :]
    pos2 = positions.reshape(n, 1)

    xn, gb = _norm_gate(x2, w_norm[None, :], w_bd, par)
    proj = _matmul(xn, w_in_p, BF16, 1024, 1024, "in_proj")

    q = _q_prep(proj, pos2, invf, g_q_latent[None, :], w_q, g_qh)
    k, v = _kv_prep(proj, pos2, invf, g_kv_latent[None, :], w_kv, g_kn, g_kr)
    o_mla = _attention(q, k, v, proj, batch, seq)

    u, w, qd, ai, kdt, gl8 = _gdn_prep(proj, conv_w, gb, batch, seq)
    o_gdn = _gdn_scan(u, w, qd, ai, kdt, gl8, proj, g_gdn_out[None, :], batch, seq)

    h = _merge(o_mla, o_gdn, w_o_mla.astype(BF16), w_o_gdn.astype(BF16), proj, b_gate)
    out = _out_proj(h, w_out.astype(BF16), x2)
    return out.reshape(batch, seq, D_MODEL)
```

```python
import functools

import jax
import jax.numpy as jnp
from jax import lax
from jax.experimental import pallas as pl
from jax.experimental.pallas import tpu as pltpu

F32 = jnp.float32
BF16 = jnp.bfloat16

D_MODEL = 4096
CHUNK = 64
RMS_EPS = 1e-6
L2_EPS = 1e-6

MLA_HEADS = 16
QK_NOPE = 128
QK_ROPE = 64
QK_HEAD = QK_NOPE + QK_ROPE
V_HEAD = 128
Q_LORA = 1024
KV_LORA = 512
ROPE_THETA = 10000.0
MLA_WIDTH = MLA_HEADS * V_HEAD

GDN_HEADS = 16
GDN_DK = 128
GDN_DV = 128
GDN_KEY_WIDTH = GDN_HEADS * GDN_DK
GDN_WIDTH = GDN_HEADS * GDN_DV
GDN_QKV = 2 * GDN_KEY_WIDTH + GDN_WIDTH
CONV_K = 4

LANES = 128
QK_PAD = 256
GDN_BLOCK = 128
GDN_GROUP = 4

OFF_GATE_MLA = 0
OFF_GATE_GDN = OFF_GATE_MLA + D_MODEL
OFF_QKV = OFF_GATE_GDN + D_MODEL
OFF_Z_MLA = OFF_QKV + GDN_QKV
OFF_Z_GDN = OFF_Z_MLA + MLA_WIDTH
OFF_CQ = OFF_Z_GDN + GDN_WIDTH
OFF_CKV = OFF_CQ + Q_LORA
OFF_KROPE = OFF_CKV + KV_LORA
PROJ_WIDTH = 20480

VMEM_LIMIT = 56 * 1024 * 1024


def _cparams(sem):
    return pltpu.CompilerParams(dimension_semantics=sem, vmem_limit_bytes=VMEM_LIMIT)


def _sigmoid(x):
    return 1.0 / (1.0 + jnp.exp(-x))


def _norm_gate_kernel(x_ref, wn_ref, wbd_ref, par_ref, xn_ref, gb_ref):
    x = x_ref[...]
    ms = jnp.mean(x * x, axis=-1, keepdims=True)
    xn = (x * lax.rsqrt(ms + RMS_EPS) * wn_ref[...]).astype(BF16)
    xn_ref[...] = xn
    bd = jnp.dot(xn, wbd_ref[...], preferred_element_type=F32)
    neg_a = -jnp.exp(par_ref[0:1, :])
    z = bd + par_ref[1:2, :]
    softplus = jnp.maximum(z, 0.0) + jnp.log1p(jnp.exp(-jnp.abs(z)))
    lane = lax.broadcasted_iota(jnp.int32, bd.shape, 1) % LANES
    gb_ref[...] = jnp.where(lane < 16, _sigmoid(bd), neg_a * softplus)


def _norm_gate(x2, w_norm, w_bd, par, tm=256):
    n = x2.shape[0]
    gw = w_bd.shape[1]
    return pl.pallas_call(
        _norm_gate_kernel,
        grid=(n // tm,),
        in_specs=[
            pl.BlockSpec((tm, D_MODEL), lambda i: (i, 0)),
            pl.BlockSpec((1, D_MODEL), lambda i: (0, 0)),
            pl.BlockSpec((D_MODEL, gw), lambda i: (0, 0)),
            pl.BlockSpec((8, gw), lambda i: (0, 0)),
        ],
        out_specs=[
            pl.BlockSpec((tm, D_MODEL), lambda i: (i, 0)),
            pl.BlockSpec((tm, gw), lambda i: (i, 0)),
        ],
        out_shape=[
            jax.ShapeDtypeStruct((n, D_MODEL), BF16),
            jax.ShapeDtypeStruct((n, gw), F32),
        ],
        compiler_params=_cparams(("parallel",)),
        name="norm_gate",
    )(x2, w_norm, w_bd, par)


def _matmul_kernel(a_ref, b_ref, o_ref):
    o_ref[...] = jnp.dot(a_ref[...], b_ref[...], preferred_element_type=F32).astype(o_ref.dtype)


def _matmul(a, b, out_dtype, tm, tn, name):
    m, k = a.shape
    n = b.shape[1]
    return pl.pallas_call(
        _matmul_kernel,
        grid=(m // tm, n // tn),
        in_specs=[
            pl.BlockSpec((tm, k), lambda i, j: (i, 0)),
            pl.BlockSpec((k, tn), lambda i, j: (0, j)),
        ],
        out_specs=pl.BlockSpec((tm, tn), lambda i, j: (i, j)),
        out_shape=jax.ShapeDtypeStruct((m, n), out_dtype),
        compiler_params=_cparams(("parallel", "parallel")),
        name=name,
    )(a, b)


def _rope_tables(pos_ref, invf_ref):
    ang = pos_ref[...].astype(F32) * invf_ref[...]
    cos = jnp.cos(ang)
    sin = jnp.sin(ang)
    lane = lax.broadcasted_iota(jnp.int32, ang.shape, 1)
    c = jnp.where(lane < QK_ROPE, cos, 0.0)
    s_lo = jnp.where(lane < QK_ROPE // 2, -sin, 0.0)
    s_hi = jnp.where((lane >= QK_ROPE // 2) & (lane < QK_ROPE), sin, 0.0)
    return c, s_lo, s_hi


def _rope(r, tables):
    c, s_lo, s_hi = tables
    half = QK_ROPE // 2
    return r * c + pltpu.roll(r, LANES - half, 1) * s_lo + pltpu.roll(r, half, 1) * s_hi


def _rms_rows(c, gain):
    ms = jnp.mean(c * c, axis=-1, keepdims=True)
    return c * lax.rsqrt(ms + RMS_EPS) * gain


def _q_prep_kernel(cq_ref, pos_ref, invf_ref, gl_ref, w_ref, gh_ref, q_ref):
    cn = _rms_rows(cq_ref[...].astype(F32), gl_ref[...]).astype(BF16)
    q = jnp.dot(cn, w_ref[...], preferred_element_type=F32)
    tables = _rope_tables(pos_ref, invf_ref)
    g_nope = gh_ref[:, :QK_NOPE]
    g_rope = gh_ref[:, QK_NOPE:]
    scale = QK_HEAD ** -0.5
    for h in range(MLA_HEADS):
        a = q[:, h * QK_PAD:h * QK_PAD + QK_NOPE]
        r = q[:, h * QK_PAD + QK_NOPE:(h + 1) * QK_PAD]
        ss = jnp.sum(a * a + r * r, axis=-1, keepdims=True)
        inv = lax.rsqrt(ss * (1.0 / QK_HEAD) + RMS_EPS) * scale
        q_ref[:, h * QK_PAD:h * QK_PAD + QK_NOPE] = (a * inv * g_nope).astype(BF16)
        q_ref[:, h * QK_PAD + QK_NOPE:(h + 1) * QK_PAD] = (_rope(r * g_rope, tables) * inv).astype(BF16)


def _q_prep(proj, pos2, invf, g_lat, w_q, g_head, tm=256):
    n = proj.shape[0]
    width = MLA_HEADS * QK_PAD
    return pl.pallas_call(
        _q_prep_kernel,
        grid=(n // tm,),
        in_specs=[
            pl.BlockSpec((tm, Q_LORA), lambda i: (i, OFF_CQ // Q_LORA)),
            pl.BlockSpec((tm, 1), lambda i: (i, 0)),
            pl.BlockSpec((1, LANES), lambda i: (0, 0)),
            pl.BlockSpec((1, Q_LORA), lambda i: (0, 0)),
            pl.BlockSpec((Q_LORA, width), lambda i: (0, 0)),
            pl.BlockSpec((1, QK_PAD), lambda i: (0, 0)),
        ],
        out_specs=pl.BlockSpec((tm, width), lambda i: (i, 0)),
        out_shape=jax.ShapeDtypeStruct((n, width), BF16),
        compiler_params=_cparams(("parallel",)),
        name="q_prep",
    )(proj, pos2, invf, g_lat, w_q, g_head)


def _kv_prep_kernel(ckv_ref, kr_ref, pos_ref, invf_ref, gl_ref, w_ref, gkn_ref, gkr_ref, k_ref, vt_ref):
    cn = _rms_rows(ckv_ref[...].astype(F32), gl_ref[...]).astype(BF16)
    kv = jnp.dot(cn, w_ref[...], preferred_element_type=F32)
    tables = _rope_tables(pos_ref, invf_ref)
    kr = kr_ref[...].astype(F32)
    ss_r = jnp.sum(kr * kr, axis=-1, keepdims=True)
    rot = _rope(kr * gkr_ref[...], tables)
    g_nope = gkn_ref[...]
    for h in range(MLA_HEADS):
        kn = kv[:, h * QK_PAD:h * QK_PAD + QK_NOPE]
        v = kv[:, h * QK_PAD + QK_NOPE:(h + 1) * QK_PAD]
        ss = jnp.sum(kn * kn, axis=-1, keepdims=True) + ss_r
        inv = lax.rsqrt(ss * (1.0 / QK_HEAD) + RMS_EPS)
        k_ref[:, h * QK_PAD:h * QK_PAD + QK_NOPE] = (kn * inv * g_nope).astype(BF16)
        k_ref[:, h * QK_PAD + QK_NOPE:(h + 1) * QK_PAD] = (rot * inv).astype(BF16)
        vt_ref[h * V_HEAD:(h + 1) * V_HEAD, :] = v.T.astype(BF16)


def _kv_prep(proj, pos2, invf, g_lat, w_kv, g_kn, g_kr, batch, seq, tm=256):
    n = proj.shape[0]
    kwidth = MLA_HEADS * QK_PAD
    ns = seq // tm
    tok = lambda width: (lambda b, i: (b * ns + i, width))
    const = lambda b, i: (0, 0)
    return pl.pallas_call(
        _kv_prep_kernel,
        grid=(batch, ns),
        in_specs=[
            pl.BlockSpec((tm, KV_LORA), tok(OFF_CKV // KV_LORA)),
            pl.BlockSpec((tm, LANES), tok(OFF_KROPE // LANES)),
            pl.BlockSpec((tm, 1), tok(0)),
            pl.BlockSpec((1, LANES), const),
            pl.BlockSpec((1, KV_LORA), const),
            pl.BlockSpec((KV_LORA, kwidth), const),
            pl.BlockSpec((1, LANES), const),
            pl.BlockSpec((1, LANES), const),
        ],
        out_specs=[
            pl.BlockSpec((tm, kwidth), tok(0)),
            pl.BlockSpec((MLA_WIDTH, tm), lambda b, i: (b, i)),
        ],
        out_shape=[
            jax.ShapeDtypeStruct((n, kwidth), BF16),
            jax.ShapeDtypeStruct((batch * MLA_WIDTH, seq), BF16),
        ],
        compiler_params=_cparams(("parallel", "parallel")),
        name="kv_prep",
    )(proj, proj, pos2, invf, g_lat, w_kv, g_kn, g_kr)


def _attn_kernel(q_ref, k_ref, vt_ref, z_ref, o_ref, *, seq, tq):
    nq = seq // tq
    key_chunk = lax.broadcasted_iota(jnp.int32, (tq, tq), 0) // CHUNK
    qry_chunk = lax.broadcasted_iota(jnp.int32, (tq, tq), 1) // CHUNK
    diag_mask = key_chunk <= qry_chunk

    def step(q, k, vt, carry, mask):
        m, l, acc = carry
        st = lax.dot_general(k, q, (((1,), (1,)), ((), ())), preferred_element_type=F32)
        if mask is not None:
            st = jnp.where(mask, st, -jnp.inf)
        m_new = jnp.maximum(m, jnp.max(st, axis=0, keepdims=True))
        alpha = jnp.exp(m - m_new)
        p = jnp.exp(st - m_new)
        l = alpha * l + jnp.sum(p, axis=0, keepdims=True)
        acc = alpha * acc + jnp.dot(vt, p.astype(BF16), preferred_element_type=F32)
        return m_new, l, acc

    for i in range(nq):
        rows = slice(i * tq, (i + 1) * tq)
        q = q_ref[rows, :]

        def body(j, carry, q=q):
            off = pl.multiple_of(j * tq, tq)
            return step(q, k_ref[pl.ds(off, tq), :], vt_ref[:, pl.ds(off, tq)], carry, None)

        carry = (jnp.full((1, tq), -1e30, F32), jnp.zeros((1, tq), F32), jnp.zeros((V_HEAD, tq), F32))
        if i > 0:
            carry = lax.fori_loop(0, i, body, carry, unroll=min(2, i))
        _, l, acc = step(q, k_ref[rows, :], vt_ref[:, rows], carry, diag_mask)
        z = z_ref[rows, :].astype(F32)
        o_ref[rows, :] = ((acc / l).T * (z * _sigmoid(z))).astype(o_ref.dtype)


def _attention(q, k, vt, proj, batch, seq, tq=512):
    n = q.shape[0]
    return pl.pallas_call(
        functools.partial(_attn_kernel, seq=seq, tq=tq),
        grid=(batch, MLA_HEADS),
        in_specs=[
            pl.BlockSpec((seq, QK_PAD), lambda b, h: (b, h)),
            pl.BlockSpec((seq, QK_PAD), lambda b, h: (b, h)),
            pl.BlockSpec((V_HEAD, seq), lambda b, h: (b * MLA_HEADS + h, 0)),
            pl.BlockSpec((seq, V_HEAD), lambda b, h: (b, OFF_Z_MLA // V_HEAD + h)),
        ],
        out_specs=pl.BlockSpec((seq, V_HEAD), lambda b, h: (b, h)),
        out_shape=jax.ShapeDtypeStruct((n, MLA_WIDTH), BF16),
        compiler_params=_cparams(("parallel", "parallel")),
        name="mla_attention",
    )(q, k, vt, proj)


def _gdn_prep_kernel(q_ref, k_ref, v_ref, qh_ref, kh_ref, vh_ref, cwq_ref, cwk_ref, cwv_ref, gb_ref,
                     u_ref, w_ref, qd_ref, ai_ref, kdt_ref, gl_ref, xe_ref, *, tt):
    first = pl.program_id(1) == 0
    blk = GDN_BLOCK
    nb = tt // blk
    pair = 2 * LANES
    halo_rows = 8

    gb = gb_ref[...]
    rmod = lax.broadcasted_iota(jnp.int32, (tt, LANES), 0) % blk
    gc = gb
    s = 1
    while s < blk:
        gc = gc + jnp.where(rmod >= s, pltpu.roll(gc, s, 0), 0.0)
        s *= 2
    gc_t = [gc[p * blk:(p + 1) * blk].T for p in range(nb)]

    def conv_silu(slot, x_ref, h_ref, cw_ref):
        halo = h_ref[...].astype(F32)[16 - halo_rows:16]
        xe_ref[slot, 0:halo_rows, :] = jnp.where(first, 0.0, halo)
        xe_ref[slot, halo_rows:, :] = x_ref[...].astype(F32)
        cw = cw_ref[...]
        base = halo_rows - (CONV_K - 1)
        y = xe_ref[slot, pl.ds(base, tt), :] * cw[0:1]
        for j in range(1, CONV_K):
            y = y + xe_ref[slot, pl.ds(base + j, tt), :] * cw[j:j + 1]
        return y * _sigmoid(y)

    def l2norm_heads(x, scale):
        parts = []
        for hh in range(GDN_GROUP):
            xs = x[:, hh * LANES:(hh + 1) * LANES]
            parts.append(xs * (lax.rsqrt(jnp.sum(xs * xs, axis=-1, keepdims=True) + L2_EPS) * scale))
        return jnp.concatenate(parts, axis=1)

    def lane_bcast(x, lane0):
        return jnp.concatenate(
            [jnp.broadcast_to(x[:, lane0 + hh:lane0 + hh + 1], (tt, LANES)) for hh in range(GDN_GROUP)], axis=1)

    q = l2norm_heads(conv_silu(0, q_ref, qh_ref, cwq_ref), GDN_DK ** -0.5)
    k = l2norm_heads(conv_silu(1, k_ref, kh_ref, cwk_ref), 1.0)
    v = conv_silu(2, v_ref, vh_ref, cwv_ref)
    beta_b = lane_bcast(gb, 0)
    gcol_b = lane_bcast(gc, 16)
    egc = jnp.exp(gcol_b)
    kb = k * beta_b
    vb = v * beta_b
    kbe = kb * egc
    qd_ref[...] = (q * egc).astype(BF16)
    glast_b = jnp.concatenate(
        [jnp.broadcast_to(gcol_b[(p + 1) * blk - 1:(p + 1) * blk, :], (blk, gcol_b.shape[1])) for p in range(nb)], axis=0)
    kd = k * jnp.exp(glast_b - gcol_b)
    for p in range(nb):
        r = slice(p * blk, (p + 1) * blk)
        gl_ref[p * 8:(p + 1) * 8, :] = jnp.exp(glast_b[p * blk:p * blk + 8, :])
        for hh in range(GDN_GROUP):
            c = slice(hh * LANES, (hh + 1) * LANES)
            kdt_ref[c, r] = kd[r, c].T.astype(BF16)

    ri = lax.broadcasted_iota(jnp.int32, (blk, pair), 0)
    ci = lax.broadcasted_iota(jnp.int32, (blk, pair), 1) % LANES
    causal = ri >= ci
    strict = ri > ci
    zero = jnp.zeros((blk, LANES), BF16)

    def bdiag(x):
        return jnp.concatenate([jnp.concatenate([x[:, :LANES], zero], axis=1),
                                jnp.concatenate([zero, x[:, LANES:]], axis=1)], axis=0)

    def mm(a, b):
        return jnp.dot(a, b, preferred_element_type=F32)

    items = [(p, j) for p in range(nb) for j in range(GDN_GROUP // 2)]
    rc = lambda p, j: (slice(p * blk, (p + 1) * blk), slice(j * pair, (j + 1) * pair))

    kq = []
    for p, j in items:
        r, c = rc(p, j)
        lhs = jnp.concatenate([kb[r, c].astype(BF16), q[r, c].astype(BF16)], axis=0)
        kq.append(lax.dot_general(lhs, bdiag(k[r, c].astype(BF16)), (((1,), (1,)), ((), ())),
                                  preferred_element_type=F32))
    e = []
    for (p, j), kq_i in zip(items, kq):
        r, c = rc(p, j)
        grow = jnp.concatenate([gc_t[p][16 + 2 * j:17 + 2 * j, :], gc_t[p][17 + 2 * j:18 + 2 * j, :]], axis=1)
        diff = gcol_b[r, c] - grow
        decay = jnp.where(causal, jnp.exp(jnp.where(causal, diff, 0.0)), 0.0)
        ai_ref[r, c] = (kq_i[blk:] * decay).astype(BF16)
        e.append(jnp.where(strict, -(kq_i[:blk] * decay), 0.0))
    eb = [x.astype(BF16) for x in e]
    pw = [mm(x, bdiag(x)) for x in eb]
    span = 2
    while span * 2 < blk:
        pwb = [x.astype(BF16) for x in pw]
        both = [mm(jnp.concatenate([e_i.astype(BF16), p_i], axis=0), bdiag(p_i)) for e_i, p_i in zip(e, pwb)]
        e = [e_i + p_i + b_i[:blk] for e_i, p_i, b_i in zip(e, pw, both)]
        pw = [b_i[blk:] for b_i in both]
        span *= 2
    last = [mm(e_i.astype(BF16), bdiag(p_i.astype(BF16))) for e_i, p_i in zip(e, pw)]
    e = [e_i + p_i + l_i for e_i, p_i, l_i in zip(e, pw, last)]
    for (p, j), e_i in zip(items, e):
        r, _ = rc(p, j)
        for s in range(2):
            c = slice((2 * j + s) * LANES, (2 * j + s + 1) * LANES)
            rhs = jnp.concatenate([vb[r, c].astype(BF16), kbe[r, c].astype(BF16)], axis=1)
            sol = mm(e_i[:, s * LANES:(s + 1) * LANES].astype(BF16), rhs)
            u_ref[r, c] = vb[r, c] + sol[:, :LANES]
            w_ref[r, c] = (kbe[r, c] + sol[:, LANES:]).astype(BF16)


def _gdn_prep(proj, conv_w, gb, batch, seq, tt=256):
    n = proj.shape[0]
    gw = GDN_GROUP * LANES
    ns = seq // tt
    ngrp = GDN_HEADS // GDN_GROUP
    col = lambda base: (lambda b, i, g: (b * ns + i, base // gw + g))
    halo = lambda base: (lambda b, i, g: (jnp.maximum((b * ns + i) * (tt // 16) - 1, 0), base // gw + g))
    cwc = lambda base: (lambda b, i, g: (0, base // gw + g))
    tok = lambda b, i, g: (b * ns + i, g)
    return pl.pallas_call(
        functools.partial(_gdn_prep_kernel, tt=tt),
        grid=(batch, ns, ngrp),
        in_specs=[
            pl.BlockSpec((tt, gw), col(OFF_QKV)),
            pl.BlockSpec((tt, gw), col(OFF_QKV + GDN_KEY_WIDTH)),
            pl.BlockSpec((tt, gw), col(OFF_QKV + 2 * GDN_KEY_WIDTH)),
            pl.BlockSpec((16, gw), halo(OFF_QKV)),
            pl.BlockSpec((16, gw), halo(OFF_QKV + GDN_KEY_WIDTH)),
            pl.BlockSpec((16, gw), halo(OFF_QKV + 2 * GDN_KEY_WIDTH)),
            pl.BlockSpec((CONV_K, gw), cwc(0)),
            pl.BlockSpec((CONV_K, gw), cwc(GDN_KEY_WIDTH)),
            pl.BlockSpec((CONV_K, gw), cwc(2 * GDN_KEY_WIDTH)),
            pl.BlockSpec((tt, LANES), tok),
        ],
        out_specs=[
            pl.BlockSpec((tt, gw), tok),
            pl.BlockSpec((tt, gw), tok),
            pl.BlockSpec((tt, gw), tok),
            pl.BlockSpec((tt, gw), tok),
            pl.BlockSpec((gw, tt), lambda b, i, g: (b * ngrp + g, i)),
            pl.BlockSpec((tt // GDN_BLOCK * 8, gw), tok),
        ],
        out_shape=[
            jax.ShapeDtypeStruct((n, GDN_WIDTH), F32),
            jax.ShapeDtypeStruct((n, GDN_WIDTH), BF16),
            jax.ShapeDtypeStruct((n, GDN_WIDTH), BF16),
            jax.ShapeDtypeStruct((n, GDN_WIDTH), BF16),
            jax.ShapeDtypeStruct((batch * GDN_HEADS * GDN_DK, seq), BF16),
            jax.ShapeDtypeStruct((n // GDN_BLOCK * 8, GDN_WIDTH), F32),
        ],
        scratch_shapes=[pltpu.VMEM((3, tt + 8, gw), F32)],
        compiler_params=_cparams(("parallel", "parallel", "parallel")),
        name="gdn_prep",
    )(proj, proj, proj, proj, proj, proj, conv_w, conv_w, conv_w, gb)


def _gdn_scan_kernel(u_ref, w_ref, qd_ref, ai_ref, kdt_ref, gl_ref, z_ref, g_ref, o_ref, s_ref, *, ts, hg):
    @pl.when(pl.program_id(2) == 0)
    def _():
        s_ref[...] = jnp.zeros_like(s_ref)

    blk = GDN_BLOCK
    gain = g_ref[...]
    for p in range(ts // blk):
        r = slice(p * blk, (p + 1) * blk)
        for hh in range(hg):
            c = slice(hh * LANES, (hh + 1) * LANES)
            state = s_ref[hh]
            lhs1 = jnp.concatenate([w_ref[r, c], qd_ref[r, c]], axis=0)
            r1 = jnp.dot(lhs1, state.astype(BF16), preferred_element_type=F32)
            v_new = u_ref[r, c] - r1[:blk]
            lhs2 = jnp.concatenate([ai_ref[r, c], kdt_ref[c, r]], axis=0)
            r2 = jnp.dot(lhs2, v_new.astype(BF16), preferred_element_type=F32)
            o = r1[blk:] + r2[:blk]
            gl = jnp.concatenate([gl_ref[p * 8:(p + 1) * 8, c]] * (blk // 8), axis=0)
            s_ref[hh] = state * gl + r2[blk:]
            z = z_ref[r, c].astype(F32)
            o_ref[r, c] = (_rms_rows(o, gain) * (z * _sigmoid(z))).astype(o_ref.dtype)


def _gdn_scan(u, w, qd, ai, kdt, gl8, proj, g_out, batch, seq, ts=512, hg=8):
    n = u.shape[0]
    gw = hg * LANES
    ns = seq // ts
    ngrp = GDN_HEADS // hg
    tok = lambda b, g, i: (b * ns + i, g)
    return pl.pallas_call(
        functools.partial(_gdn_scan_kernel, ts=ts, hg=hg),
        grid=(batch, ngrp, ns),
        in_specs=[
            pl.BlockSpec((ts, gw), tok),
            pl.BlockSpec((ts, gw), tok),
            pl.BlockSpec((ts, gw), tok),
            pl.BlockSpec((ts, gw), tok),
            pl.BlockSpec((gw, ts), lambda b, g, i: (b * ngrp + g, i)),
            pl.BlockSpec((ts // GDN_BLOCK * 8, gw), tok),
            pl.BlockSpec((ts, gw), lambda b, g, i: (b * ns + i, OFF_Z_GDN // gw + g)),
            pl.BlockSpec((1, LANES), lambda b, g, i: (0, 0)),
        ],
        out_specs=pl.BlockSpec((ts, gw), tok),
        out_shape=jax.ShapeDtypeStruct((n, GDN_WIDTH), BF16),
        scratch_shapes=[pltpu.VMEM((hg, GDN_DK, GDN_DV), F32)],
        compiler_params=_cparams(("parallel", "parallel", "arbitrary")),
        name="gdn_scan",
    )(u, w, qd, ai, kdt, gl8, proj, g_out)


def _merge_kernel(a1_ref, a2_ref, w1_ref, w2_ref, g1_ref, g2_ref, b_ref, h_ref):
    y1 = jnp.dot(a1_ref[...], w1_ref[...], preferred_element_type=F32)
    y2 = jnp.dot(a2_ref[...], w2_ref[...], preferred_element_type=F32)
    s1 = _sigmoid(g1_ref[...].astype(F32) + b_ref[0:1, :])
    s2 = _sigmoid(g2_ref[...].astype(F32) + b_ref[1:2, :])
    h_ref[...] = (s1 * y1 + s2 * y2).astype(h_ref.dtype)


def _merge(o_mla, o_gdn, w_o_mla, w_o_gdn, proj, b_gate, tm=512, tn=1024):
    n = o_mla.shape[0]
    return pl.pallas_call(
        _merge_kernel,
        grid=(n // tm, D_MODEL // tn),
        in_specs=[
            pl.BlockSpec((tm, MLA_WIDTH), lambda i, j: (i, 0)),
            pl.BlockSpec((tm, GDN_WIDTH), lambda i, j: (i, 0)),
            pl.BlockSpec((MLA_WIDTH, tn), lambda i, j: (0, j)),
            pl.BlockSpec((GDN_WIDTH, tn), lambda i, j: (0, j)),
            pl.BlockSpec((tm, tn), lambda i, j: (i, OFF_GATE_MLA // tn + j)),
            pl.BlockSpec((tm, tn), lambda i, j: (i, OFF_GATE_GDN // tn + j)),
            pl.BlockSpec((2, tn), lambda i, j: (0, j)),
        ],
        out_specs=pl.BlockSpec((tm, tn), lambda i, j: (i, j)),
        out_shape=jax.ShapeDtypeStruct((n, D_MODEL), BF16),
        compiler_params=_cparams(("parallel", "parallel")),
        name="merge",
    )(o_mla, o_gdn, w_o_mla, w_o_gdn, proj, proj, b_gate)


def _out_kernel(h_ref, w_ref, x_ref, o_ref):
    o_ref[...] = x_ref[...] + jnp.dot(h_ref[...], w_ref[...], preferred_element_type=F32)


def _out_proj(h, w_out, x2, tm=1024, tn=1024):
    n = h.shape[0]
    return pl.pallas_call(
        _out_kernel,
        grid=(n // tm, D_MODEL // tn),
        in_specs=[
            pl.BlockSpec((tm, D_MODEL), lambda i, j: (i, 0)),
            pl.BlockSpec((D_MODEL, tn), lambda i, j: (0, j)),
            pl.BlockSpec((tm, tn), lambda i, j: (i, j)),
        ],
        out_specs=pl.BlockSpec((tm, tn), lambda i, j: (i, j)),
        out_shape=jax.ShapeDtypeStruct((n, D_MODEL), F32),
        compiler_params=_cparams(("parallel", "parallel")),
        name="out_proj",
    )(h, w_out, x2)


def _pad_cols(a, width):
    return jnp.pad(a, ((0, 0), (0, width - a.shape[1])))


def kernel(x, positions, w_norm, w_in, b_gate, w_q_up, w_kv_up, g_q_latent, g_kv_latent, g_q_head, g_k_head,
           w_o_mla, conv_w, a_log, dt_bias, g_gdn_out, w_o_gdn, w_out):
    batch, seq, _ = x.shape
    n = batch * seq
    x2 = x.reshape(n, D_MODEL)

    o_cq, o_ckv, o_kr = 0, Q_LORA, Q_LORA + KV_LORA
    o_zm = o_kr + QK_ROPE
    o_qkv = o_zm + MLA_WIDTH
    o_b = o_qkv + GDN_QKV
    o_a = o_b + GDN_HEADS
    o_zg = o_a + GDN_HEADS
    o_gm = o_zg + GDN_WIDTH
    o_gg = o_gm + D_MODEL
    seg = lambda lo, width: w_in[:, lo:lo + width]
    w_in_p = jnp.concatenate([
        seg(o_gm, D_MODEL), seg(o_gg, D_MODEL), seg(o_qkv, GDN_QKV), seg(o_zm, MLA_WIDTH), seg(o_zg, GDN_WIDTH),
        seg(o_cq, Q_LORA), seg(o_ckv, KV_LORA), seg(o_kr, QK_ROPE),
        jnp.zeros((D_MODEL, PROJ_WIDTH - OFF_KROPE - QK_ROPE), w_in.dtype)], axis=1).astype(BF16)

    ngrp = GDN_HEADS // GDN_GROUP

    def gate_cols(beta_part, decay_part, fill_rows):
        cols = []
        for g in range(ngrp):
            hs = slice(g * GDN_GROUP, (g + 1) * GDN_GROUP)
            cols += [beta_part[:, hs], jnp.zeros((fill_rows, 16 - GDN_GROUP), F32),
                     decay_part[:, hs], jnp.zeros((fill_rows, LANES - 16 - GDN_GROUP), F32)]
        return jnp.concatenate(cols, axis=1)

    w_bd = gate_cols(seg(o_b, GDN_HEADS), seg(o_a, GDN_HEADS), D_MODEL).astype(BF16)
    zeros_h = jnp.zeros((1, GDN_HEADS), F32)
    par = jnp.concatenate([gate_cols(zeros_h, a_log[None, :], 1), gate_cols(zeros_h, dt_bias[None, :], 1),
                           jnp.zeros((6, ngrp * LANES), F32)], axis=0)

    w_q = jnp.pad(w_q_up.reshape(Q_LORA, MLA_HEADS, QK_HEAD), ((0, 0), (0, 0), (0, QK_PAD - QK_HEAD)))
    w_q = w_q.reshape(Q_LORA, MLA_HEADS * QK_PAD).astype(BF16)
    w_kv = w_kv_up.astype(BF16)
    g_qh = _pad_cols(g_q_head[None, :], QK_PAD)
    g_kn = g_k_head[None, :QK_NOPE]
    g_kr = _pad_cols(g_k_head[None, QK_NOPE:], LANES)
    inv_freq = ROPE_THETA ** (-jnp.arange(0, QK_ROPE, 2, dtype=F32) / QK_ROPE)
    invf = jnp.tile(inv_freq, LANES // (QK_ROPE // 2))[None, :]
    pos2 = positions.reshape(n, 1)

    xn, gb = _norm_gate(x2, w_norm[None, :], w_bd, par)
    proj = _matmul(xn, w_in_p, BF16, 1024, 1024, "in_proj")

    q = _q_prep(proj, pos2, invf, g_q_latent[None, :], w_q, g_qh)
    k, vt = _kv_prep(proj, pos2, invf, g_kv_latent[None, :], w_kv, g_kn, g_kr, batch, seq)
    o_mla = _attention(q, k, vt, proj, batch, seq)

    u, w, qd, ai, kdt, gl8 = _gdn_prep(proj, conv_w, gb, batch, seq)
    o_gdn = _gdn_scan(u, w, qd, ai, kdt, gl8, proj, g_gdn_out[None, :], batch, seq)

    h = _merge(o_mla, o_gdn, w_o_mla.astype(BF16), w_o_gdn.astype(BF16), proj, b_gate)
    out = _out_proj(h, w_out.astype(BF16), x2)
    return out.reshape(batch, seq, D_MODEL)
```

```python
import functools

import jax
import jax.numpy as jnp
from jax import lax
from jax.experimental import pallas as pl
from jax.experimental.pallas import tpu as pltpu

F32 = jnp.float32
BF16 = jnp.bfloat16

D_MODEL = 4096
CHUNK = 64
RMS_EPS = 1e-6
L2_EPS = 1e-6

MLA_HEADS = 16
QK_NOPE = 128
QK_ROPE = 64
QK_HEAD = QK_NOPE + QK_ROPE
V_HEAD = 128
Q_LORA = 1024
KV_LORA = 512
ROPE_THETA = 10000.0
MLA_WIDTH = MLA_HEADS * V_HEAD

GDN_HEADS = 16
GDN_DK = 128
GDN_DV = 128
GDN_KEY_WIDTH = GDN_HEADS * GDN_DK
GDN_WIDTH = GDN_HEADS * GDN_DV
GDN_QKV = 2 * GDN_KEY_WIDTH + GDN_WIDTH
CONV_K = 4

LANES = 128
QK_PAD = 256
GDN_BLOCK = 128
GDN_GROUP = 4

OFF_GATE_MLA = 0
OFF_GATE_GDN = OFF_GATE_MLA + D_MODEL
OFF_QKV = OFF_GATE_GDN + D_MODEL
OFF_Z_MLA = OFF_QKV + GDN_QKV
OFF_Z_GDN = OFF_Z_MLA + MLA_WIDTH
OFF_CQ = OFF_Z_GDN + GDN_WIDTH
OFF_CKV = OFF_CQ + Q_LORA
OFF_KROPE = OFF_CKV + KV_LORA
PROJ_WIDTH = 20480

VMEM_LIMIT = 56 * 1024 * 1024
LOG2E = 1.4426950408889634


def _cparams(sem):
    return pltpu.CompilerParams(dimension_semantics=sem, vmem_limit_bytes=VMEM_LIMIT)


def _sigmoid(x):
    return 1.0 / (1.0 + jnp.exp(-x))


def _norm_gate_kernel(x_ref, wn_ref, wbd_ref, par_ref, xn_ref, gb_ref):
    x = x_ref[...]
    ms = jnp.mean(x * x, axis=-1, keepdims=True)
    xn = (x * lax.rsqrt(ms + RMS_EPS) * wn_ref[...]).astype(BF16)
    xn_ref[...] = xn
    bd = jnp.dot(xn, wbd_ref[...], preferred_element_type=F32)
    neg_a = -jnp.exp(par_ref[0:1, :])
    z = bd + par_ref[1:2, :]
    softplus = jnp.maximum(z, 0.0) + jnp.log1p(jnp.exp(-jnp.abs(z)))
    lane = lax.broadcasted_iota(jnp.int32, bd.shape, 1) % LANES
    gb_ref[...] = jnp.where(lane < 16, _sigmoid(bd), neg_a * softplus)


def _norm_gate(x2, w_norm, w_bd, par, tm=256):
    n = x2.shape[0]
    gw = w_bd.shape[1]
    return pl.pallas_call(
        _norm_gate_kernel,
        grid=(n // tm,),
        in_specs=[
            pl.BlockSpec((tm, D_MODEL), lambda i: (i, 0)),
            pl.BlockSpec((1, D_MODEL), lambda i: (0, 0)),
            pl.BlockSpec((D_MODEL, gw), lambda i: (0, 0)),
            pl.BlockSpec((8, gw), lambda i: (0, 0)),
        ],
        out_specs=[
            pl.BlockSpec((tm, D_MODEL), lambda i: (i, 0)),
            pl.BlockSpec((tm, gw), lambda i: (i, 0)),
        ],
        out_shape=[
            jax.ShapeDtypeStruct((n, D_MODEL), BF16),
            jax.ShapeDtypeStruct((n, gw), F32),
        ],
        compiler_params=_cparams(("parallel",)),
        name="norm_gate",
    )(x2, w_norm, w_bd, par)


def _matmul_kernel(a_ref, b_ref, o_ref):
    o_ref[...] = jnp.dot(a_ref[...], b_ref[...], preferred_element_type=F32).astype(o_ref.dtype)


def _matmul(a, b, out_dtype, tm, tn, name):
    m, k = a.shape
    n = b.shape[1]
    return pl.pallas_call(
        _matmul_kernel,
        grid=(m // tm, n // tn),
        in_specs=[
            pl.BlockSpec((tm, k), lambda i, j: (i, 0)),
            pl.BlockSpec((k, tn), lambda i, j: (0, j)),
        ],
        out_specs=pl.BlockSpec((tm, tn), lambda i, j: (i, j)),
        out_shape=jax.ShapeDtypeStruct((m, n), out_dtype),
        compiler_params=_cparams(("parallel", "parallel")),
        name=name,
    )(a, b)


def _rope_tables(pos_ref, invf_ref):
    ang = pos_ref[...].astype(F32) * invf_ref[...]
    cos = jnp.cos(ang)
    sin = jnp.sin(ang)
    lane = lax.broadcasted_iota(jnp.int32, ang.shape, 1)
    c = jnp.where(lane < QK_ROPE, cos, 0.0)
    s_lo = jnp.where(lane < QK_ROPE // 2, -sin, 0.0)
    s_hi = jnp.where((lane >= QK_ROPE // 2) & (lane < QK_ROPE), sin, 0.0)
    return c, s_lo, s_hi


def _rope(r, tables):
    c, s_lo, s_hi = tables
    half = QK_ROPE // 2
    return r * c + pltpu.roll(r, LANES - half, 1) * s_lo + pltpu.roll(r, half, 1) * s_hi


def _rms_rows(c, gain):
    ms = jnp.mean(c * c, axis=-1, keepdims=True)
    return c * lax.rsqrt(ms + RMS_EPS) * gain


def _mla_prep_kernel(cq_ref, ckv_ref, kr_ref, pos_ref, invf_ref, glq_ref, glkv_ref, wq_ref, wkn_ref, wvt_ref,
                     gqh_ref, gkn_ref, gkr_ref, q_ref, k_ref, vt_ref):
    heads = range(MLA_HEADS)
    tables = _rope_tables(pos_ref, invf_ref)
    cq = _rms_rows(cq_ref[...].astype(F32), glq_ref[...]).astype(BF16)
    ckv = _rms_rows(ckv_ref[...].astype(F32), glkv_ref[...]).astype(BF16)
    q = jnp.dot(cq, wq_ref[...], preferred_element_type=F32)
    k_nope = jnp.dot(ckv, wkn_ref[...], preferred_element_type=F32)
    vt_ref[...] = lax.dot_general(wvt_ref[...], ckv, (((1,), (1,)), ((), ())),
                                  preferred_element_type=F32).astype(BF16)
    kr = kr_ref[...].astype(F32)
    ss_r = jnp.sum(kr * kr, axis=-1, keepdims=True)
    k_rot = _rope(kr * gkr_ref[...], tables)
    g_qn = gqh_ref[:, :QK_NOPE]
    g_qr = gqh_ref[:, QK_NOPE:]
    g_kn = gkn_ref[...]
    scale = QK_HEAD ** -0.5 * LOG2E

    qa = [q[:, h * QK_PAD:h * QK_PAD + QK_NOPE] for h in heads]
    qr = [q[:, h * QK_PAD + QK_NOPE:(h + 1) * QK_PAD] for h in heads]
    kn = [k_nope[:, h * QK_NOPE:(h + 1) * QK_NOPE] for h in heads]
    ss_q = [jnp.sum(a * a + r * r, axis=-1, keepdims=True) for a, r in zip(qa, qr)]
    ss_k = [jnp.sum(x * x, axis=-1, keepdims=True) + ss_r for x in kn]
    inv_q = [lax.rsqrt(s * (1.0 / QK_HEAD) + RMS_EPS) * scale for s in ss_q]
    inv_k = [lax.rsqrt(s * (1.0 / QK_HEAD) + RMS_EPS) for s in ss_k]
    q_rot = [_rope(r * g_qr, tables) for r in qr]
    for h in heads:
        q_ref[:, h * QK_PAD:h * QK_PAD + QK_NOPE] = (qa[h] * inv_q[h] * g_qn).astype(BF16)
        q_ref[:, h * QK_PAD + QK_NOPE:(h + 1) * QK_PAD] = (q_rot[h] * inv_q[h]).astype(BF16)
        k_ref[:, h * QK_PAD:h * QK_PAD + QK_NOPE] = (kn[h] * inv_k[h] * g_kn).astype(BF16)
        k_ref[:, h * QK_PAD + QK_NOPE:(h + 1) * QK_PAD] = (k_rot * inv_k[h]).astype(BF16)


def _mla_prep(proj, pos2, invf, g_q_lat, g_kv_lat, w_q, w_kn, w_vt, g_qh, g_kn, g_kr, batch, seq, tm=256):
    n = proj.shape[0]
    width = MLA_HEADS * QK_PAD
    ns = seq // tm
    tok = lambda col: (lambda b, i: (b * ns + i, col))
    const = lambda b, i: (0, 0)
    return pl.pallas_call(
        _mla_prep_kernel,
        grid=(batch, ns),
        in_specs=[
            pl.BlockSpec((tm, Q_LORA), tok(OFF_CQ // Q_LORA)),
            pl.BlockSpec((tm, KV_LORA), tok(OFF_CKV // KV_LORA)),
            pl.BlockSpec((tm, LANES), tok(OFF_KROPE // LANES)),
            pl.BlockSpec((tm, 1), tok(0)),
            pl.BlockSpec((1, LANES), const),
            pl.BlockSpec((1, Q_LORA), const),
            pl.BlockSpec((1, KV_LORA), const),
            pl.BlockSpec((Q_LORA, width), const),
            pl.BlockSpec((KV_LORA, MLA_HEADS * QK_NOPE), const),
            pl.BlockSpec((MLA_WIDTH, KV_LORA), const),
            pl.BlockSpec((1, QK_PAD), const),
            pl.BlockSpec((1, LANES), const),
            pl.BlockSpec((1, LANES), const),
        ],
        out_specs=[
            pl.BlockSpec((tm, width), tok(0)),
            pl.BlockSpec((tm, width), tok(0)),
            pl.BlockSpec((MLA_WIDTH, tm), lambda b, i: (b, i)),
        ],
        out_shape=[
            jax.ShapeDtypeStruct((n, width), BF16),
            jax.ShapeDtypeStruct((n, width), BF16),
            jax.ShapeDtypeStruct((batch * MLA_WIDTH, seq), BF16),
        ],
        compiler_params=_cparams(("parallel", "parallel")),
        name="mla_prep",
    )(proj, proj, proj, pos2, invf, g_q_lat, g_kv_lat, w_q, w_kn, w_vt, g_qh, g_kn, g_kr)


def _attn_kernel(q_ref, k_ref, vt_ref, z_ref, o_ref, *, seq, tq):
    nq = seq // tq
    hq = tq // 2
    key_chunk = lax.broadcasted_iota(jnp.int32, (tq, hq), 0) // CHUNK
    qry_chunk = lax.broadcasted_iota(jnp.int32, (tq, hq), 1) // CHUNK
    diag_masks = tuple(key_chunk <= qry_chunk + (x * hq) // CHUNK for x in range(2))

    def qk(q, k):
        return lax.dot_general(k, q, (((1,), (1,)), ((), ())), preferred_element_type=F32)

    def softmax_step(st, carry, mask):
        m, l, acc = carry
        if mask is not None:
            st = jnp.where(mask, st, -jnp.inf)
        m_new = jnp.maximum(m, jnp.max(st, axis=0, keepdims=True))
        alpha = jnp.exp2(m - m_new)
        p = jnp.exp2(st - m_new)
        l = alpha * l + jnp.sum(p, axis=0, keepdims=True)
        return p.astype(BF16), alpha, (m_new, l, acc)

    def pv(vt, p, alpha, carry):
        m, l, acc = carry
        return m, l, alpha * acc + jnp.dot(vt, p, preferred_element_type=F32)

    def one_tile(qs, tile, carries):
        k, vt, masks = tile
        sx = qk(qs[0], k)
        sy = qk(qs[1], k)
        px, ax, cx = softmax_step(sx, carries[0], masks and masks[0])
        py, ay, cy = softmax_step(sy, carries[1], masks and masks[1])
        return pv(vt, px, ax, cx), pv(vt, py, ay, cy)

    def two_tiles(qs, tile_a, tile_b, carries):
        (ka, va, ma), (kb, vb, mb) = tile_a, tile_b
        qx, qy = qs
        cx, cy = carries
        sx0 = qk(qx, ka)
        sy0 = qk(qy, ka)
        px0, ax0, cx = softmax_step(sx0, cx, ma and ma[0])
        sx1 = qk(qx, kb)
        py0, ay0, cy = softmax_step(sy0, cy, ma and ma[1])
        cx = pv(va, px0, ax0, cx)
        sy1 = qk(qy, kb)
        px1, ax1, cx = softmax_step(sx1, cx, mb and mb[0])
        cy = pv(va, py0, ay0, cy)
        py1, ay1, cy = softmax_step(sy1, cy, mb and mb[1])
        cx = pv(vb, px1, ax1, cx)
        cy = pv(vb, py1, ay1, cy)
        return cx, cy

    def tile_at(off, masks):
        return k_ref[pl.ds(off, tq), :], vt_ref[:, pl.ds(off, tq)], masks

    for i in range(nq):
        qs = (q_ref[i * tq:i * tq + hq, :], q_ref[i * tq + hq:(i + 1) * tq, :])

        def body(t, carries, qs=qs):
            off = pl.multiple_of(t * (2 * tq), 2 * tq)
            return two_tiles(qs, tile_at(off, None), tile_at(off + tq, None), carries)

        init = (jnp.full((1, hq), -1e30, F32), jnp.zeros((1, hq), F32), jnp.zeros((V_HEAD, hq), F32))
        carries = (init, init)
        if i >= 2:
            carries = lax.fori_loop(0, i // 2, body, carries)
        diag = tile_at(i * tq, diag_masks)
        if i % 2 == 1:
            carries = two_tiles(qs, tile_at((i - 1) * tq, None), diag, carries)
        else:
            carries = one_tile(qs, diag, carries)
        for x in range(2):
            rows = slice(i * tq + x * hq, i * tq + (x + 1) * hq)
            _, l, acc = carries[x]
            z = z_ref[rows, :].astype(F32)
            o_ref[rows, :] = ((acc / l).T * (z * _sigmoid(z))).astype(o_ref.dtype)


def _attention(q, k, vt, proj, batch, seq, tq=512):
    n = q.shape[0]
    return pl.pallas_call(
        functools.partial(_attn_kernel, seq=seq, tq=tq),
        grid=(batch, MLA_HEADS),
        in_specs=[
            pl.BlockSpec((seq, QK_PAD), lambda b, h: (b, h)),
            pl.BlockSpec((seq, QK_PAD), lambda b, h: (b, h)),
            pl.BlockSpec((V_HEAD, seq), lambda b, h: (b * MLA_HEADS + h, 0)),
            pl.BlockSpec((seq, V_HEAD), lambda b, h: (b, OFF_Z_MLA // V_HEAD + h)),
        ],
        out_specs=pl.BlockSpec((seq, V_HEAD), lambda b, h: (b, h)),
        out_shape=jax.ShapeDtypeStruct((n, MLA_WIDTH), BF16),
        compiler_params=_cparams(("parallel", "parallel")),
        name="mla_attention",
    )(q, k, vt, proj)


def _gdn_prep_kernel(q_ref, k_ref, v_ref, qh_ref, kh_ref, vh_ref, cwq_ref, cwk_ref, cwv_ref, gb_ref,
                     u_ref, w_ref, qd_ref, ai_ref, kdt_ref, gl_ref, *, tt):
    first = pl.program_id(1) == 0
    blk = GDN_BLOCK
    nb = tt // blk
    pair = 2 * LANES
    halo_rows = 8

    gb = gb_ref[...]
    rmod = lax.broadcasted_iota(jnp.int32, (tt, LANES), 0) % blk
    gc = gb
    s = 1
    while s < blk:
        gc = gc + jnp.where(rmod >= s, pltpu.roll(gc, s, 0), 0.0)
        s *= 2
    gc_t = [gc[p * blk:(p + 1) * blk].T for p in range(nb)]

    def conv_silu(x_ref, h_ref, cw_ref):
        halo = h_ref[...].astype(F32)[16 - halo_rows:16]
        xe = jnp.concatenate([jnp.where(first, 0.0, halo), x_ref[...].astype(F32)], axis=0)
        cw = cw_ref[...]
        y = xe * cw[CONV_K - 1:CONV_K]
        for sh in range(1, CONV_K):
            y = y + pltpu.roll(xe, sh, 0) * cw[CONV_K - 1 - sh:CONV_K - sh]
        y = y[halo_rows:]
        return y * _sigmoid(y)

    def l2norm_heads(x, scale):
        parts = []
        for hh in range(GDN_GROUP):
            xs = x[:, hh * LANES:(hh + 1) * LANES]
            parts.append(xs * (lax.rsqrt(jnp.sum(xs * xs, axis=-1, keepdims=True) + L2_EPS) * scale))
        return jnp.concatenate(parts, axis=1)

    def lane_bcast(x, lane0):
        return jnp.concatenate(
            [jnp.broadcast_to(x[:, lane0 + hh:lane0 + hh + 1], (tt, LANES)) for hh in range(GDN_GROUP)], axis=1)

    q = l2norm_heads(conv_silu(q_ref, qh_ref, cwq_ref), GDN_DK ** -0.5)
    k = l2norm_heads(conv_silu(k_ref, kh_ref, cwk_ref), 1.0)
    v = conv_silu(v_ref, vh_ref, cwv_ref)
    beta_b = lane_bcast(gb, 0)
    gcol_b = lane_bcast(gc, 16)
    egc = jnp.exp(gcol_b)
    kb = k * beta_b
    vb = v * beta_b
    kbe = kb * egc
    qd_ref[...] = (q * egc).astype(BF16)
    glast_b = jnp.concatenate(
        [jnp.broadcast_to(gcol_b[(p + 1) * blk - 1:(p + 1) * blk, :], (blk, gcol_b.shape[1])) for p in range(nb)], axis=0)
    kd = k * jnp.exp(glast_b - gcol_b)
    for p in range(nb):
        r = slice(p * blk, (p + 1) * blk)
        gl_ref[p * 8:(p + 1) * 8, :] = jnp.exp(glast_b[p * blk:p * blk + 8, :])
        for hh in range(GDN_GROUP):
            c = slice(hh * LANES, (hh + 1) * LANES)
            kdt_ref[c, r] = kd[r, c].T.astype(BF16)

    ri = lax.broadcasted_iota(jnp.int32, (blk, pair), 0)
    ci = lax.broadcasted_iota(jnp.int32, (blk, pair), 1) % LANES
    causal = ri >= ci
    strict = ri > ci
    zero = jnp.zeros((blk, LANES), BF16)

    def bdiag(x):
        return jnp.concatenate([jnp.concatenate([x[:, :LANES], zero], axis=1),
                                jnp.concatenate([zero, x[:, LANES:]], axis=1)], axis=0)

    def mm(a, b):
        return jnp.dot(a, b, preferred_element_type=F32)

    items = [(p, j) for p in range(nb) for j in range(GDN_GROUP // 2)]
    rc = lambda p, j: (slice(p * blk, (p + 1) * blk), slice(j * pair, (j + 1) * pair))

    kq = []
    for p, j in items:
        r, c = rc(p, j)
        lhs = jnp.concatenate([kb[r, c].astype(BF16), q[r, c].astype(BF16)], axis=0)
        kq.append(lax.dot_general(lhs, bdiag(k[r, c].astype(BF16)), (((1,), (1,)), ((), ())),
                                  preferred_element_type=F32))
    e = []
    for (p, j), kq_i in zip(items, kq):
        r, c = rc(p, j)
        grow = jnp.concatenate([gc_t[p][16 + 2 * j:17 + 2 * j, :], gc_t[p][17 + 2 * j:18 + 2 * j, :]], axis=1)
        diff = gcol_b[r, c] - grow
        decay = jnp.where(causal, jnp.exp(jnp.where(causal, diff, 0.0)), 0.0)
        ai_ref[r, c] = (kq_i[blk:] * decay).astype(BF16)
        e.append(jnp.where(strict, -(kq_i[:blk] * decay), 0.0))
    eb = [x.astype(BF16) for x in e]
    pw = [mm(x, bdiag(x)) for x in eb]
    span = 2
    while span * 2 < blk:
        pwb = [x.astype(BF16) for x in pw]
        both = [mm(jnp.concatenate([e_i.astype(BF16), p_i], axis=0), bdiag(p_i)) for e_i, p_i in zip(e, pwb)]
        e = [e_i + p_i + b_i[:blk] for e_i, p_i, b_i in zip(e, pw, both)]
        pw = [b_i[blk:] for b_i in both]
        span *= 2
    last = [mm(e_i.astype(BF16), bdiag(p_i.astype(BF16))) for e_i, p_i in zip(e, pw)]
    e = [e_i + p_i + l_i for e_i, p_i, l_i in zip(e, pw, last)]
    for (p, j), e_i in zip(items, e):
        r, _ = rc(p, j)
        for s in range(2):
            c = slice((2 * j + s) * LANES, (2 * j + s + 1) * LANES)
            rhs = jnp.concatenate([vb[r, c].astype(BF16), kbe[r, c].astype(BF16)], axis=1)
            sol = mm(e_i[:, s * LANES:(s + 1) * LANES].astype(BF16), rhs)
            u_ref[r, c] = vb[r, c] + sol[:, :LANES]
            w_ref[r, c] = (kbe[r, c] + sol[:, LANES:]).astype(BF16)


def _gdn_prep(proj, conv_w, gb, batch, seq, tt=256):
    n = proj.shape[0]
    gw = GDN_GROUP * LANES
    ns = seq // tt
    ngrp = GDN_HEADS // GDN_GROUP
    col = lambda base: (lambda b, i, g: (b * ns + i, base // gw + g))
    halo = lambda base: (lambda b, i, g: (jnp.maximum((b * ns + i) * (tt // 16) - 1, 0), base // gw + g))
    cwc = lambda base: (lambda b, i, g: (0, base // gw + g))
    tok = lambda b, i, g: (b * ns + i, g)
    return pl.pallas_call(
        functools.partial(_gdn_prep_kernel, tt=tt),
        grid=(batch, ns, ngrp),
        in_specs=[
            pl.BlockSpec((tt, gw), col(OFF_QKV)),
            pl.BlockSpec((tt, gw), col(OFF_QKV + GDN_KEY_WIDTH)),
            pl.BlockSpec((tt, gw), col(OFF_QKV + 2 * GDN_KEY_WIDTH)),
            pl.BlockSpec((16, gw), halo(OFF_QKV)),
            pl.BlockSpec((16, gw), halo(OFF_QKV + GDN_KEY_WIDTH)),
            pl.BlockSpec((16, gw), halo(OFF_QKV + 2 * GDN_KEY_WIDTH)),
            pl.BlockSpec((CONV_K, gw), cwc(0)),
            pl.BlockSpec((CONV_K, gw), cwc(GDN_KEY_WIDTH)),
            pl.BlockSpec((CONV_K, gw), cwc(2 * GDN_KEY_WIDTH)),
            pl.BlockSpec((tt, LANES), tok),
        ],
        out_specs=[
            pl.BlockSpec((tt, gw), tok),
            pl.BlockSpec((tt, gw), tok),
            pl.BlockSpec((tt, gw), tok),
            pl.BlockSpec((tt, gw), tok),
            pl.BlockSpec((gw, tt), lambda b, i, g: (b * ngrp + g, i)),
            pl.BlockSpec((tt // GDN_BLOCK * 8, gw), tok),
        ],
        out_shape=[
            jax.ShapeDtypeStruct((n, GDN_WIDTH), F32),
            jax.ShapeDtypeStruct((n, GDN_WIDTH), BF16),
            jax.ShapeDtypeStruct((n, GDN_WIDTH), BF16),
            jax.ShapeDtypeStruct((n, GDN_WIDTH), BF16),
            jax.ShapeDtypeStruct((batch * GDN_HEADS * GDN_DK, seq), BF16),
            jax.ShapeDtypeStruct((n // GDN_BLOCK * 8, GDN_WIDTH), F32),
        ],
        compiler_params=_cparams(("parallel", "parallel", "parallel")),
        name="gdn_prep",
    )(proj, proj, proj, proj, proj, proj, conv_w, conv_w, conv_w, gb)


def _gdn_scan_kernel(u_ref, w_ref, qd_ref, ai_ref, kdt_ref, gl_ref, z_ref, g_ref, o_ref, s_ref, *, ts, hg):
    @pl.when(pl.program_id(2) == 0)
    def _():
        s_ref[...] = jnp.zeros_like(s_ref)

    blk = GDN_BLOCK
    gain = g_ref[...]
    for p in range(ts // blk):
        r = slice(p * blk, (p + 1) * blk)
        for hh in range(hg):
            c = slice(hh * LANES, (hh + 1) * LANES)
            state = s_ref[hh]
            lhs1 = jnp.concatenate([w_ref[r, c], qd_ref[r, c]], axis=0)
            r1 = jnp.dot(lhs1, state.astype(BF16), preferred_element_type=F32)
            v_new = u_ref[r, c] - r1[:blk]
            lhs2 = jnp.concatenate([ai_ref[r, c], kdt_ref[c, r]], axis=0)
            r2 = jnp.dot(lhs2, v_new.astype(BF16), preferred_element_type=F32)
            o = r1[blk:] + r2[:blk]
            gl = jnp.concatenate([gl_ref[p * 8:(p + 1) * 8, c]] * (blk // 8), axis=0)
            s_ref[hh] = state * gl + r2[blk:]
            z = z_ref[r, c].astype(F32)
            o_ref[r, c] = (_rms_rows(o, gain) * (z * _sigmoid(z))).astype(o_ref.dtype)


def _gdn_scan(u, w, qd, ai, kdt, gl8, proj, g_out, batch, seq, ts=512, hg=8):
    n = u.shape[0]
    gw = hg * LANES
    ns = seq // ts
    ngrp = GDN_HEADS // hg
    tok = lambda b, g, i: (b * ns + i, g)
    return pl.pallas_call(
        functools.partial(_gdn_scan_kernel, ts=ts, hg=hg),
        grid=(batch, ngrp, ns),
        in_specs=[
            pl.BlockSpec((ts, gw), tok),
            pl.BlockSpec((ts, gw), tok),
            pl.BlockSpec((ts, gw), tok),
            pl.BlockSpec((ts, gw), tok),
            pl.BlockSpec((gw, ts), lambda b, g, i: (b * ngrp + g, i)),
            pl.BlockSpec((ts // GDN_BLOCK * 8, gw), tok),
            pl.BlockSpec((ts, gw), lambda b, g, i: (b * ns + i, OFF_Z_GDN // gw + g)),
            pl.BlockSpec((1, LANES), lambda b, g, i: (0, 0)),
        ],
        out_specs=pl.BlockSpec((ts, gw), tok),
        out_shape=jax.ShapeDtypeStruct((n, GDN_WIDTH), BF16),
        scratch_shapes=[pltpu.VMEM((hg, GDN_DK, GDN_DV), F32)],
        compiler_params=_cparams(("parallel", "parallel", "arbitrary")),
        name="gdn_scan",
    )(u, w, qd, ai, kdt, gl8, proj, g_out)


def _merge_kernel(a1_ref, a2_ref, w1_ref, w2_ref, g1_ref, g2_ref, b_ref, h_ref):
    y1 = jnp.dot(a1_ref[...], w1_ref[...], preferred_element_type=F32)
    y2 = jnp.dot(a2_ref[...], w2_ref[...], preferred_element_type=F32)
    s1 = _sigmoid(g1_ref[...].astype(F32) + b_ref[0:1, :])
    s2 = _sigmoid(g2_ref[...].astype(F32) + b_ref[1:2, :])
    h_ref[...] = (s1 * y1 + s2 * y2).astype(h_ref.dtype)


def _merge(o_mla, o_gdn, w_o_mla, w_o_gdn, proj, b_gate, tm=512, tn=1024):
    n = o_mla.shape[0]
    return pl.pallas_call(
        _merge_kernel,
        grid=(n // tm, D_MODEL // tn),
        in_specs=[
            pl.BlockSpec((tm, MLA_WIDTH), lambda i, j: (i, 0)),
            pl.BlockSpec((tm, GDN_WIDTH), lambda i, j: (i, 0)),
            pl.BlockSpec((MLA_WIDTH, tn), lambda i, j: (0, j)),
            pl.BlockSpec((GDN_WIDTH, tn), lambda i, j: (0, j)),
            pl.BlockSpec((tm, tn), lambda i, j: (i, OFF_GATE_MLA // tn + j)),
            pl.BlockSpec((tm, tn), lambda i, j: (i, OFF_GATE_GDN // tn + j)),
            pl.BlockSpec((2, tn), lambda i, j: (0, j)),
        ],
        out_specs=pl.BlockSpec((tm, tn), lambda i, j: (i, j)),
        out_shape=jax.ShapeDtypeStruct((n, D_MODEL), BF16),
        compiler_params=_cparams(("parallel", "parallel")),
        name="merge",
    )(o_mla, o_gdn, w_o_mla, w_o_gdn, proj, proj, b_gate)


def _out_kernel(h_ref, w_ref, x_ref, o_ref):
    o_ref[...] = x_ref[...] + jnp.dot(h_ref[...], w_ref[...], preferred_element_type=F32)


def _out_proj(h, w_out, x2, tm=1024, tn=1024):
    n = h.shape[0]
    return pl.pallas_call(
        _out_kernel,
        grid=(n // tm, D_MODEL // tn),
        in_specs=[
            pl.BlockSpec((tm, D_MODEL), lambda i, j: (i, 0)),
            pl.BlockSpec((D_MODEL, tn), lambda i, j: (0, j)),
            pl.BlockSpec((tm, tn), lambda i, j: (i, j)),
        ],
        out_specs=pl.BlockSpec((tm, tn), lambda i, j: (i, j)),
        out_shape=jax.ShapeDtypeStruct((n, D_MODEL), F32),
        compiler_params=_cparams(("parallel", "parallel")),
        name="out_proj",
    )(h, w_out, x2)


def _relayout_kernel(w_ref, o_ref, *, segments):
    dst = 0
    for src, width in segments:
        o_ref[:, dst:dst + width] = w_ref[:, src:src + width].astype(o_ref.dtype)
        dst += width
    o_ref[:, dst:] = jnp.zeros((o_ref.shape[0], o_ref.shape[1] - dst), o_ref.dtype)


def _relayout(w_in, segments, rows=128):
    k, width = w_in.shape
    return pl.pallas_call(
        functools.partial(_relayout_kernel, segments=segments),
        grid=(k // rows,),
        in_specs=[pl.BlockSpec((rows, width), lambda i: (i, 0))],
        out_specs=pl.BlockSpec((rows, PROJ_WIDTH), lambda i: (i, 0)),
        out_shape=jax.ShapeDtypeStruct((k, PROJ_WIDTH), BF16),
        compiler_params=_cparams(("parallel",)),
        name="w_in_relayout",
    )(w_in)


def _pad_cols(a, width):
    return jnp.pad(a, ((0, 0), (0, width - a.shape[1])))


def kernel(x, positions, w_norm, w_in, b_gate, w_q_up, w_kv_up, g_q_latent, g_kv_latent, g_q_head, g_k_head,
           w_o_mla, conv_w, a_log, dt_bias, g_gdn_out, w_o_gdn, w_out):
    batch, seq, _ = x.shape
    n = batch * seq
    x2 = x.reshape(n, D_MODEL)

    o_cq, o_ckv, o_kr = 0, Q_LORA, Q_LORA + KV_LORA
    o_zm = o_kr + QK_ROPE
    o_qkv = o_zm + MLA_WIDTH
    o_b = o_qkv + GDN_QKV
    o_a = o_b + GDN_HEADS
    o_zg = o_a + GDN_HEADS
    o_gm = o_zg + GDN_WIDTH
    o_gg = o_gm + D_MODEL
    seg = lambda lo, width: w_in[:, lo:lo + width]
    w_in_p = _relayout(w_in, ((o_gm, D_MODEL), (o_gg, D_MODEL), (o_qkv, GDN_QKV), (o_zm, MLA_WIDTH), (o_zg, GDN_WIDTH),
                              (o_cq, Q_LORA), (o_ckv, KV_LORA), (o_kr, QK_ROPE)))

    ngrp = GDN_HEADS // GDN_GROUP

    def gate_cols(beta_part, decay_part, fill_rows):
        cols = []
        for g in range(ngrp):
            hs = slice(g * GDN_GROUP, (g + 1) * GDN_GROUP)
            cols += [beta_part[:, hs], jnp.zeros((fill_rows, 16 - GDN_GROUP), F32),
                     decay_part[:, hs], jnp.zeros((fill_rows, LANES - 16 - GDN_GROUP), F32)]
        return jnp.concatenate(cols, axis=1)

    w_bd = gate_cols(seg(o_b, GDN_HEADS), seg(o_a, GDN_HEADS), D_MODEL).astype(BF16)
    zeros_h = jnp.zeros((1, GDN_HEADS), F32)
    par = jnp.concatenate([gate_cols(zeros_h, a_log[None, :], 1), gate_cols(zeros_h, dt_bias[None, :], 1),
                           jnp.zeros((6, ngrp * LANES), F32)], axis=0)

    w_q = jnp.pad(w_q_up.reshape(Q_LORA, MLA_HEADS, QK_HEAD), ((0, 0), (0, 0), (0, QK_PAD - QK_HEAD)))
    w_q = w_q.reshape(Q_LORA, MLA_HEADS * QK_PAD).astype(BF16)
    w_kv3 = w_kv_up.reshape(KV_LORA, MLA_HEADS, QK_NOPE + V_HEAD)
    w_kn = w_kv3[:, :, :QK_NOPE].reshape(KV_LORA, MLA_HEADS * QK_NOPE).astype(BF16)
    w_vt = w_kv3[:, :, QK_NOPE:].reshape(KV_LORA, MLA_WIDTH).T.astype(BF16)
    g_qh = _pad_cols(g_q_head[None, :], QK_PAD)
    g_kn = g_k_head[None, :QK_NOPE]
    g_kr = _pad_cols(g_k_head[None, QK_NOPE:], LANES)
    inv_freq = ROPE_THETA ** (-jnp.arange(0, QK_ROPE, 2, dtype=F32) / QK_ROPE)
    invf = jnp.tile(inv_freq, LANES // (QK_ROPE // 2))[None, :]
    pos2 = positions.reshape(n, 1)

    xn, gb = _norm_gate(x2, w_norm[None, :], w_bd, par)
    proj = _matmul(xn, w_in_p, BF16, 1024, 1024, "in_proj")

    q, k, vt = _mla_prep(proj, pos2, invf, g_q_latent[None, :], g_kv_latent[None, :], w_q, w_kn, w_vt, g_qh, g_kn,
                         g_kr, batch, seq)
    o_mla = _attention(q, k, vt, proj, batch, seq)

    u, w, qd, ai, kdt, gl8 = _gdn_prep(proj, conv_w, gb, batch, seq)
    o_gdn = _gdn_scan(u, w, qd, ai, kdt, gl8, proj, g_gdn_out[None, :], batch, seq)

    h = _merge(o_mla, o_gdn, w_o_mla.astype(BF16), w_o_gdn.astype(BF16), proj, b_gate)
    out = _out_proj(h, w_out.astype(BF16), x2)
    return out.reshape(batch, seq, D_MODEL)
```

```python
import functools

import jax
import jax.numpy as jnp
from jax import lax
from jax.experimental import pallas as pl
from jax.experimental.pallas import tpu as pltpu

F32 = jnp.float32
BF16 = jnp.bfloat16

D_MODEL = 4096
CHUNK = 64
RMS_EPS = 1e-6
L2_EPS = 1e-6

MLA_HEADS = 16
QK_NOPE = 128
QK_ROPE = 64
QK_HEAD = QK_NOPE + QK_ROPE
V_HEAD = 128
Q_LORA = 1024
KV_LORA = 512
ROPE_THETA = 10000.0
MLA_WIDTH = MLA_HEADS * V_HEAD

GDN_HEADS = 16
GDN_DK = 128
GDN_DV = 128
GDN_KEY_WIDTH = GDN_HEADS * GDN_DK
GDN_WIDTH = GDN_HEADS * GDN_DV
GDN_QKV = 2 * GDN_KEY_WIDTH + GDN_WIDTH
CONV_K = 4

LANES = 128
QK_PAD = 256
GDN_BLOCK = 128
GDN_GROUP = 4

OFF_GATE_MLA = 0
OFF_GATE_GDN = OFF_GATE_MLA + D_MODEL
OFF_QKV = OFF_GATE_GDN + D_MODEL
OFF_Z_MLA = OFF_QKV + GDN_QKV
OFF_Z_GDN = OFF_Z_MLA + MLA_WIDTH
OFF_CQ = OFF_Z_GDN + GDN_WIDTH
OFF_CKV = OFF_CQ + Q_LORA
OFF_KROPE = OFF_CKV + KV_LORA
PROJ_WIDTH = 20480

VMEM_LIMIT = 56 * 1024 * 1024
LOG2E = 1.4426950408889634


def _cparams(sem):
    return pltpu.CompilerParams(dimension_semantics=sem, vmem_limit_bytes=VMEM_LIMIT)


def _sigmoid(x):
    return 1.0 / (1.0 + jnp.exp(-x))


def _norm_gate_kernel(x_ref, wn_ref, wbd_ref, par_ref, xn_ref, gb_ref):
    x = x_ref[...]
    ms = jnp.mean(x * x, axis=-1, keepdims=True)
    xn = (x * lax.rsqrt(ms + RMS_EPS) * wn_ref[...]).astype(BF16)
    xn_ref[...] = xn
    bd = lax.dot_general(xn, wbd_ref[...], (((1,), (1,)), ((), ())), preferred_element_type=F32)
    neg_a = -jnp.exp(par_ref[0:1, :])
    z = bd + par_ref[1:2, :]
    softplus = jnp.maximum(z, 0.0) + jnp.log1p(jnp.exp(-jnp.abs(z)))
    lane = lax.broadcasted_iota(jnp.int32, bd.shape, 1) % LANES
    gb_ref[...] = jnp.where(lane < 16, _sigmoid(bd), neg_a * softplus)


def _norm_gate(x2, w_norm, w_bd_t, par, tm=256):
    n = x2.shape[0]
    gw = w_bd_t.shape[0]
    return pl.pallas_call(
        _norm_gate_kernel,
        grid=(n // tm,),
        in_specs=[
            pl.BlockSpec((tm, D_MODEL), lambda i: (i, 0)),
            pl.BlockSpec((1, D_MODEL), lambda i: (0, 0)),
            pl.BlockSpec((gw, D_MODEL), lambda i: (0, 0)),
            pl.BlockSpec((8, gw), lambda i: (0, 0)),
        ],
        out_specs=[
            pl.BlockSpec((tm, D_MODEL), lambda i: (i, 0)),
            pl.BlockSpec((tm, gw), lambda i: (i, 0)),
        ],
        out_shape=[
            jax.ShapeDtypeStruct((n, D_MODEL), BF16),
            jax.ShapeDtypeStruct((n, gw), F32),
        ],
        compiler_params=_cparams(("parallel",)),
        name="norm_gate",
    )(x2, w_norm, w_bd_t, par)


def _matmul_nt_kernel(a_ref, bt_ref, o_ref):
    o_ref[...] = lax.dot_general(a_ref[...], bt_ref[...], (((1,), (1,)), ((), ())),
                                 preferred_element_type=F32).astype(o_ref.dtype)


def _matmul_nt(a, b_t, out_dtype, tm, tn, name):
    m, k = a.shape
    n = b_t.shape[0]
    return pl.pallas_call(
        _matmul_nt_kernel,
        grid=(m // tm, n // tn),
        in_specs=[
            pl.BlockSpec((tm, k), lambda i, j: (i, 0)),
            pl.BlockSpec((tn, k), lambda i, j: (j, 0)),
        ],
        out_specs=pl.BlockSpec((tm, tn), lambda i, j: (i, j)),
        out_shape=jax.ShapeDtypeStruct((m, n), out_dtype),
        compiler_params=_cparams(("parallel", "parallel")),
        name=name,
    )(a, b_t)


def _rope_tables(pos_ref, invf_ref):
    ang = pos_ref[...].astype(F32) * invf_ref[...]
    cos = jnp.cos(ang)
    sin = jnp.sin(ang)
    lane = lax.broadcasted_iota(jnp.int32, ang.shape, 1)
    c = jnp.where(lane < QK_ROPE, cos, 0.0)
    s_lo = jnp.where(lane < QK_ROPE // 2, -sin, 0.0)
    s_hi = jnp.where((lane >= QK_ROPE // 2) & (lane < QK_ROPE), sin, 0.0)
    return c, s_lo, s_hi


def _rope(r, tables):
    c, s_lo, s_hi = tables
    half = QK_ROPE // 2
    return r * c + pltpu.roll(r, LANES - half, 1) * s_lo + pltpu.roll(r, half, 1) * s_hi


def _rms_rows(c, gain):
    ms = jnp.mean(c * c, axis=-1, keepdims=True)
    return c * lax.rsqrt(ms + RMS_EPS) * gain


def _mla_prep_kernel(cq_ref, ckv_ref, kr_ref, pos_ref, invf_ref, glq_ref, glkv_ref, wq_ref, wkn_ref, wvt_ref,
                     gqh_ref, gkn_ref, gkr_ref, q_ref, k_ref, vt_ref):
    heads = range(MLA_HEADS)
    tables = _rope_tables(pos_ref, invf_ref)
    cq = _rms_rows(cq_ref[...].astype(F32), glq_ref[...]).astype(BF16)
    ckv = _rms_rows(ckv_ref[...].astype(F32), glkv_ref[...]).astype(BF16)
    q = jnp.dot(cq, wq_ref[...], preferred_element_type=F32)
    k_nope = jnp.dot(ckv, wkn_ref[...], preferred_element_type=F32)
    vt_ref[...] = lax.dot_general(wvt_ref[...], ckv, (((1,), (1,)), ((), ())),
                                  preferred_element_type=F32).astype(BF16)
    kr = kr_ref[...].astype(F32)
    ss_r = jnp.sum(kr * kr, axis=-1, keepdims=True)
    k_rot = _rope(kr * gkr_ref[...], tables)
    g_qn = gqh_ref[:, :QK_NOPE]
    g_qr = gqh_ref[:, QK_NOPE:]
    g_kn = gkn_ref[...]
    scale = QK_HEAD ** -0.5 * LOG2E

    qa = [q[:, h * QK_PAD:h * QK_PAD + QK_NOPE] for h in heads]
    qr = [q[:, h * QK_PAD + QK_NOPE:(h + 1) * QK_PAD] for h in heads]
    kn = [k_nope[:, h * QK_NOPE:(h + 1) * QK_NOPE] for h in heads]
    ss_q = [jnp.sum(a * a + r * r, axis=-1, keepdims=True) for a, r in zip(qa, qr)]
    ss_k = [jnp.sum(x * x, axis=-1, keepdims=True) + ss_r for x in kn]
    inv_q = [lax.rsqrt(s * (1.0 / QK_HEAD) + RMS_EPS) * scale for s in ss_q]
    inv_k = [lax.rsqrt(s * (1.0 / QK_HEAD) + RMS_EPS) for s in ss_k]
    q_rot = [_rope(r * g_qr, tables) for r in qr]
    for h in heads:
        q_ref[:, h * QK_PAD:h * QK_PAD + QK_NOPE] = (qa[h] * inv_q[h] * g_qn).astype(BF16)
        q_ref[:, h * QK_PAD + QK_NOPE:(h + 1) * QK_PAD] = (q_rot[h] * inv_q[h]).astype(BF16)
        k_ref[:, h * QK_PAD:h * QK_PAD + QK_NOPE] = (kn[h] * inv_k[h] * g_kn).astype(BF16)
        k_ref[:, h * QK_PAD + QK_NOPE:(h + 1) * QK_PAD] = (k_rot * inv_k[h]).astype(BF16)


def _mla_prep(proj, pos2, invf, g_q_lat, g_kv_lat, w_q, w_kn, w_vt, g_qh, g_kn, g_kr, batch, seq, tm=256):
    n = proj.shape[0]
    width = MLA_HEADS * QK_PAD
    ns = seq // tm
    tok = lambda col: (lambda b, i: (b * ns + i, col))
    const = lambda b, i: (0, 0)
    return pl.pallas_call(
        _mla_prep_kernel,
        grid=(batch, ns),
        in_specs=[
            pl.BlockSpec((tm, Q_LORA), tok(OFF_CQ // Q_LORA)),
            pl.BlockSpec((tm, KV_LORA), tok(OFF_CKV // KV_LORA)),
            pl.BlockSpec((tm, LANES), tok(OFF_KROPE // LANES)),
            pl.BlockSpec((tm, 1), tok(0)),
            pl.BlockSpec((1, LANES), const),
            pl.BlockSpec((1, Q_LORA), const),
            pl.BlockSpec((1, KV_LORA), const),
            pl.BlockSpec((Q_LORA, width), const),
            pl.BlockSpec((KV_LORA, MLA_HEADS * QK_NOPE), const),
            pl.BlockSpec((MLA_WIDTH, KV_LORA), const),
            pl.BlockSpec((1, QK_PAD), const),
            pl.BlockSpec((1, LANES), const),
            pl.BlockSpec((1, LANES), const),
        ],
        out_specs=[
            pl.BlockSpec((tm, width), tok(0)),
            pl.BlockSpec((tm, width), tok(0)),
            pl.BlockSpec((MLA_WIDTH, tm), lambda b, i: (b, i)),
        ],
        out_shape=[
            jax.ShapeDtypeStruct((n, width), BF16),
            jax.ShapeDtypeStruct((n, width), BF16),
            jax.ShapeDtypeStruct((batch * MLA_WIDTH, seq), BF16),
        ],
        compiler_params=_cparams(("parallel", "parallel")),
        name="mla_prep",
    )(proj, proj, proj, pos2, invf, g_q_lat, g_kv_lat, w_q, w_kn, w_vt, g_qh, g_kn, g_kr)


def _attn_kernel(q_ref, k_ref, vt_ref, z_ref, o_ref, *, seq, tq):
    nq = seq // tq
    hq = tq // 2

    def chunk_mask(n_keys, query_offset):
        key_chunk = lax.broadcasted_iota(jnp.int32, (n_keys, hq), 0) // CHUNK
        qry_chunk = (lax.broadcasted_iota(jnp.int32, (n_keys, hq), 1) + query_offset) // CHUNK
        return key_chunk <= qry_chunk

    diag_mask_x = chunk_mask(hq, 0)
    diag_mask_y = chunk_mask(tq, hq)

    def qk(q, k):
        return lax.dot_general(k, q, (((1,), (1,)), ((), ())), preferred_element_type=F32)

    def softmax_step(st, carry, mask):
        m, l, acc = carry
        if mask is not None:
            st = jnp.where(mask, st, -jnp.inf)
        m_new = jnp.maximum(m, jnp.max(st, axis=0, keepdims=True))
        alpha = jnp.exp2(m - m_new)
        p = jnp.exp2(st - m_new)
        l = alpha * l + jnp.sum(p, axis=0, keepdims=True)
        return p.astype(BF16), alpha, (m_new, l, acc)

    def pv(vt, p, alpha, carry):
        m, l, acc = carry
        return m, l, alpha * acc + jnp.dot(vt, p, preferred_element_type=F32)

    def one_tile(qs, tile, carries):
        (kx, vx, mx), (ky, vy, my) = tile
        sx = qk(qs[0], kx)
        sy = qk(qs[1], ky)
        px, ax, cx = softmax_step(sx, carries[0], mx)
        py, ay, cy = softmax_step(sy, carries[1], my)
        return pv(vx, px, ax, cx), pv(vy, py, ay, cy)

    def two_tiles(qs, tile_a, tile_b, carries):
        (kxa, vxa, mxa), (kya, vya, mya) = tile_a
        (kxb, vxb, mxb), (kyb, vyb, myb) = tile_b
        qx, qy = qs
        cx, cy = carries
        sx0 = qk(qx, kxa)
        sy0 = qk(qy, kya)
        px0, ax0, cx = softmax_step(sx0, cx, mxa)
        sx1 = qk(qx, kxb)
        py0, ay0, cy = softmax_step(sy0, cy, mya)
        cx = pv(vxa, px0, ax0, cx)
        sy1 = qk(qy, kyb)
        px1, ax1, cx = softmax_step(sx1, cx, mxb)
        cy = pv(vya, py0, ay0, cy)
        py1, ay1, cy = softmax_step(sy1, cy, myb)
        cx = pv(vxb, px1, ax1, cx)
        cy = pv(vyb, py1, ay1, cy)
        return cx, cy

    def full_tile(off):
        half = (k_ref[pl.ds(off, tq), :], vt_ref[:, pl.ds(off, tq)], None)
        return half, half

    def diag_tile(off):
        return ((k_ref[pl.ds(off, hq), :], vt_ref[:, pl.ds(off, hq)], diag_mask_x),
                (k_ref[pl.ds(off, tq), :], vt_ref[:, pl.ds(off, tq)], diag_mask_y))

    for i in range(nq):
        qs = (q_ref[i * tq:i * tq + hq, :], q_ref[i * tq + hq:(i + 1) * tq, :])

        def body(t, carries, qs=qs):
            off = pl.multiple_of(t * (2 * tq), 2 * tq)
            return two_tiles(qs, full_tile(off), full_tile(off + tq), carries)

        init = (jnp.full((1, hq), -1e30, F32), jnp.zeros((1, hq), F32), jnp.zeros((V_HEAD, hq), F32))
        carries = (init, init)
        if i >= 2:
            carries = lax.fori_loop(0, i // 2, body, carries)
        diag = diag_tile(i * tq)
        if i % 2 == 1:
            carries = two_tiles(qs, full_tile((i - 1) * tq), diag, carries)
        else:
            carries = one_tile(qs, diag, carries)
        for x in range(2):
            rows = slice(i * tq + x * hq, i * tq + (x + 1) * hq)
            _, l, acc = carries[x]
            z = z_ref[rows, :].astype(F32)
            o_ref[rows, :] = ((acc / l).T * (z * _sigmoid(z))).astype(o_ref.dtype)


def _attention(q, k, vt, proj, batch, seq, tq=512):
    n = q.shape[0]
    return pl.pallas_call(
        functools.partial(_attn_kernel, seq=seq, tq=tq),
        grid=(batch, MLA_HEADS),
        in_specs=[
            pl.BlockSpec((seq, QK_PAD), lambda b, h: (b, h)),
            pl.BlockSpec((seq, QK_PAD), lambda b, h: (b, h)),
            pl.BlockSpec((V_HEAD, seq), lambda b, h: (b * MLA_HEADS + h, 0)),
            pl.BlockSpec((seq, V_HEAD), lambda b, h: (b, OFF_Z_MLA // V_HEAD + h)),
        ],
        out_specs=pl.BlockSpec((seq, V_HEAD), lambda b, h: (b, h)),
        out_shape=jax.ShapeDtypeStruct((n, MLA_WIDTH), BF16),
        compiler_params=_cparams(("parallel", "parallel")),
        name="mla_attention",
    )(q, k, vt, proj)


def _gdn_prep_kernel(q_ref, k_ref, v_ref, qh_ref, kh_ref, vh_ref, cwq_ref, cwk_ref, cwv_ref, gb_ref,
                     u_ref, w_ref, qd_ref, ai_ref, kdt_ref, gl_ref, *, tt):
    first = pl.program_id(1) == 0
    blk = GDN_BLOCK
    nb = tt // blk
    pair = 2 * LANES
    halo_rows = 8

    gb = gb_ref[...]
    rmod = lax.broadcasted_iota(jnp.int32, (tt, LANES), 0) % blk
    gc = gb
    s = 1
    while s < blk:
        gc = gc + jnp.where(rmod >= s, pltpu.roll(gc, s, 0), 0.0)
        s *= 2
    gc_t = [gc[p * blk:(p + 1) * blk].T for p in range(nb)]

    def conv_silu(x_ref, h_ref, cw_ref):
        halo = h_ref[...].astype(F32)[16 - halo_rows:16]
        xe = jnp.concatenate([jnp.where(first, 0.0, halo), x_ref[...].astype(F32)], axis=0)
        cw = cw_ref[...]
        y = xe * cw[CONV_K - 1:CONV_K]
        for sh in range(1, CONV_K):
            y = y + pltpu.roll(xe, sh, 0) * cw[CONV_K - 1 - sh:CONV_K - sh]
        y = y[halo_rows:]
        return y * _sigmoid(y)

    def l2norm_heads(x, scale):
        parts = []
        for hh in range(GDN_GROUP):
            xs = x[:, hh * LANES:(hh + 1) * LANES]
            parts.append(xs * (lax.rsqrt(jnp.sum(xs * xs, axis=-1, keepdims=True) + L2_EPS) * scale))
        return jnp.concatenate(parts, axis=1)

    def lane_bcast(x, lane0):
        return jnp.concatenate(
            [jnp.broadcast_to(x[:, lane0 + hh:lane0 + hh + 1], (tt, LANES)) for hh in range(GDN_GROUP)], axis=1)

    q = l2norm_heads(conv_silu(q_ref, qh_ref, cwq_ref), GDN_DK ** -0.5)
    k = l2norm_heads(conv_silu(k_ref, kh_ref, cwk_ref), 1.0)
    v = conv_silu(v_ref, vh_ref, cwv_ref)
    beta_b = lane_bcast(gb, 0)
    gcol_b = lane_bcast(gc, 16)
    egc = jnp.exp(gcol_b)
    kb = k * beta_b
    vb = v * beta_b
    kbe = kb * egc
    qd_ref[...] = (q * egc).astype(BF16)
    glast_b = jnp.concatenate(
        [jnp.broadcast_to(gcol_b[(p + 1) * blk - 1:(p + 1) * blk, :], (blk, gcol_b.shape[1])) for p in range(nb)], axis=0)
    kd = k * jnp.exp(glast_b - gcol_b)
    for p in range(nb):
        r = slice(p * blk, (p + 1) * blk)
        gl_ref[p * 8:(p + 1) * 8, :] = jnp.exp(glast_b[p * blk:p * blk + 8, :])
        for hh in range(GDN_GROUP):
            c = slice(hh * LANES, (hh + 1) * LANES)
            kdt_ref[c, r] = kd[r, c].T.astype(BF16)

    ri = lax.broadcasted_iota(jnp.int32, (blk, pair), 0)
    ci = lax.broadcasted_iota(jnp.int32, (blk, pair), 1) % LANES
    causal = ri >= ci
    strict = ri > ci
    zero = jnp.zeros((blk, LANES), BF16)

    def bdiag(x):
        return jnp.concatenate([jnp.concatenate([x[:, :LANES], zero], axis=1),
                                jnp.concatenate([zero, x[:, LANES:]], axis=1)], axis=0)

    def mm(a, b):
        return jnp.dot(a, b, preferred_element_type=F32)

    items = [(p, j) for p in range(nb) for j in range(GDN_GROUP // 2)]
    rc = lambda p, j: (slice(p * blk, (p + 1) * blk), slice(j * pair, (j + 1) * pair))

    kq = []
    for p, j in items:
        r, c = rc(p, j)
        lhs = jnp.concatenate([kb[r, c].astype(BF16), q[r, c].astype(BF16)], axis=0)
        kq.append(lax.dot_general(lhs, bdiag(k[r, c].astype(BF16)), (((1,), (1,)), ((), ())),
                                  preferred_element_type=F32))
    e = []
    for (p, j), kq_i in zip(items, kq):
        r, c = rc(p, j)
        grow = jnp.concatenate([gc_t[p][16 + 2 * j:17 + 2 * j, :], gc_t[p][17 + 2 * j:18 + 2 * j, :]], axis=1)
        diff = gcol_b[r, c] - grow
        decay = jnp.where(causal, jnp.exp(jnp.where(causal, diff, 0.0)), 0.0)
        ai_ref[r, c] = (kq_i[blk:] * decay).astype(BF16)
        e.append(jnp.where(strict, -(kq_i[:blk] * decay), 0.0))
    eb = [x.astype(BF16) for x in e]
    pw = [mm(x, bdiag(x)) for x in eb]
    span = 2
    while span * 2 < blk:
        pwb = [x.astype(BF16) for x in pw]
        both = [mm(jnp.concatenate([e_i.astype(BF16), p_i], axis=0), bdiag(p_i)) for e_i, p_i in zip(e, pwb)]
        e = [e_i + p_i + b_i[:blk] for e_i, p_i, b_i in zip(e, pw, both)]
        pw = [b_i[blk:] for b_i in both]
        span *= 2
    last = [mm(e_i.astype(BF16), bdiag(p_i.astype(BF16))) for e_i, p_i in zip(e, pw)]
    e = [e_i + p_i + l_i for e_i, p_i, l_i in zip(e, pw, last)]
    for (p, j), e_i in zip(items, e):
        r, _ = rc(p, j)
        for s in range(2):
            c = slice((2 * j + s) * LANES, (2 * j + s + 1) * LANES)
            rhs = jnp.concatenate([vb[r, c].astype(BF16), kbe[r, c].astype(BF16)], axis=1)
            sol = mm(e_i[:, s * LANES:(s + 1) * LANES].astype(BF16), rhs)
            u_ref[r, c] = vb[r, c] + sol[:, :LANES]
            w_ref[r, c] = (kbe[r, c] + sol[:, LANES:]).astype(BF16)


def _gdn_prep(proj, conv_w, gb, batch, seq, tt=512):
    n = proj.shape[0]
    gw = GDN_GROUP * LANES
    ns = seq // tt
    ngrp = GDN_HEADS // GDN_GROUP
    col = lambda base: (lambda b, i, g: (b * ns + i, base // gw + g))
    halo = lambda base: (lambda b, i, g: (jnp.maximum((b * ns + i) * (tt // 16) - 1, 0), base // gw + g))
    cwc = lambda base: (lambda b, i, g: (0, base // gw + g))
    tok = lambda b, i, g: (b * ns + i, g)
    return pl.pallas_call(
        functools.partial(_gdn_prep_kernel, tt=tt),
        grid=(batch, ns, ngrp),
        in_specs=[
            pl.BlockSpec((tt, gw), col(OFF_QKV)),
            pl.BlockSpec((tt, gw), col(OFF_QKV + GDN_KEY_WIDTH)),
            pl.BlockSpec((tt, gw), col(OFF_QKV + 2 * GDN_KEY_WIDTH)),
            pl.BlockSpec((16, gw), halo(OFF_QKV)),
            pl.BlockSpec((16, gw), halo(OFF_QKV + GDN_KEY_WIDTH)),
            pl.BlockSpec((16, gw), halo(OFF_QKV + 2 * GDN_KEY_WIDTH)),
            pl.BlockSpec((CONV_K, gw), cwc(0)),
            pl.BlockSpec((CONV_K, gw), cwc(GDN_KEY_WIDTH)),
            pl.BlockSpec((CONV_K, gw), cwc(2 * GDN_KEY_WIDTH)),
            pl.BlockSpec((tt, LANES), tok),
        ],
        out_specs=[
            pl.BlockSpec((tt, gw), tok),
            pl.BlockSpec((tt, gw), tok),
            pl.BlockSpec((tt, gw), tok),
            pl.BlockSpec((tt, gw), tok),
            pl.BlockSpec((gw, tt), lambda b, i, g: (b * ngrp + g, i)),
            pl.BlockSpec((tt // GDN_BLOCK * 8, gw), tok),
        ],
        out_shape=[
            jax.ShapeDtypeStruct((n, GDN_WIDTH), F32),
            jax.ShapeDtypeStruct((n, GDN_WIDTH), BF16),
            jax.ShapeDtypeStruct((n, GDN_WIDTH), BF16),
            jax.ShapeDtypeStruct((n, GDN_WIDTH), BF16),
            jax.ShapeDtypeStruct((batch * GDN_HEADS * GDN_DK, seq), BF16),
            jax.ShapeDtypeStruct((n // GDN_BLOCK * 8, GDN_WIDTH), F32),
        ],
        compiler_params=_cparams(("parallel", "parallel", "parallel")),
        name="gdn_prep",
    )(proj, proj, proj, proj, proj, proj, conv_w, conv_w, conv_w, gb)


def _gdn_scan_kernel(u_ref, w_ref, qd_ref, ai_ref, kdt_ref, gl_ref, z_ref, g_ref, o_ref, s_ref, *, ts, hg):
    @pl.when(pl.program_id(2) == 0)
    def _():
        s_ref[...] = jnp.zeros_like(s_ref)

    blk = GDN_BLOCK
    gain = g_ref[...]
    for p in range(ts // blk):
        r = slice(p * blk, (p + 1) * blk)
        for hh in range(hg):
            c = slice(hh * LANES, (hh + 1) * LANES)
            state = s_ref[hh]
            lhs1 = jnp.concatenate([w_ref[r, c], qd_ref[r, c]], axis=0)
            r1 = jnp.dot(lhs1, state.astype(BF16), preferred_element_type=F32)
            v_new = u_ref[r, c] - r1[:blk]
            lhs2 = jnp.concatenate([ai_ref[r, c], kdt_ref[c, r]], axis=0)
            r2 = jnp.dot(lhs2, v_new.astype(BF16), preferred_element_type=F32)
            o = r1[blk:] + r2[:blk]
            gl = jnp.concatenate([gl_ref[p * 8:(p + 1) * 8, c]] * (blk // 8), axis=0)
            s_ref[hh] = state * gl + r2[blk:]
            z = z_ref[r, c].astype(F32)
            o_ref[r, c] = (_rms_rows(o, gain) * (z * _sigmoid(z))).astype(o_ref.dtype)


def _gdn_scan(u, w, qd, ai, kdt, gl8, proj, g_out, batch, seq, ts=512, hg=8):
    n = u.shape[0]
    gw = hg * LANES
    ns = seq // ts
    ngrp = GDN_HEADS // hg
    tok = lambda b, g, i: (b * ns + i, g)
    return pl.pallas_call(
        functools.partial(_gdn_scan_kernel, ts=ts, hg=hg),
        grid=(batch, ngrp, ns),
        in_specs=[
            pl.BlockSpec((ts, gw), tok),
            pl.BlockSpec((ts, gw), tok),
            pl.BlockSpec((ts, gw), tok),
            pl.BlockSpec((ts, gw), tok),
            pl.BlockSpec((gw, ts), lambda b, g, i: (b * ngrp + g, i)),
            pl.BlockSpec((ts // GDN_BLOCK * 8, gw), tok),
            pl.BlockSpec((ts, gw), lambda b, g, i: (b * ns + i, OFF_Z_GDN // gw + g)),
            pl.BlockSpec((1, LANES), lambda b, g, i: (0, 0)),
        ],
        out_specs=pl.BlockSpec((ts, gw), tok),
        out_shape=jax.ShapeDtypeStruct((n, GDN_WIDTH), BF16),
        scratch_shapes=[pltpu.VMEM((hg, GDN_DK, GDN_DV), F32)],
        compiler_params=_cparams(("parallel", "parallel", "arbitrary")),
        name="gdn_scan",
    )(u, w, qd, ai, kdt, gl8, proj, g_out)


def _merge_kernel(a1_ref, a2_ref, w1_ref, w2_ref, g1_ref, g2_ref, b_ref, h_ref):
    y1 = jnp.dot(a1_ref[...], w1_ref[...], preferred_element_type=F32)
    y2 = jnp.dot(a2_ref[...], w2_ref[...], preferred_element_type=F32)
    s1 = _sigmoid(g1_ref[...].astype(F32) + b_ref[0:1, :])
    s2 = _sigmoid(g2_ref[...].astype(F32) + b_ref[1:2, :])
    h_ref[...] = (s1 * y1 + s2 * y2).astype(h_ref.dtype)


def _merge(o_mla, o_gdn, w_o_mla, w_o_gdn, proj, b_gate, tm=1024, tn=1024):
    n = o_mla.shape[0]
    return pl.pallas_call(
        _merge_kernel,
        grid=(n // tm, D_MODEL // tn),
        in_specs=[
            pl.BlockSpec((tm, MLA_WIDTH), lambda i, j: (i, 0)),
            pl.BlockSpec((tm, GDN_WIDTH), lambda i, j: (i, 0)),
            pl.BlockSpec((MLA_WIDTH, tn), lambda i, j: (0, j)),
            pl.BlockSpec((GDN_WIDTH, tn), lambda i, j: (0, j)),
            pl.BlockSpec((tm, tn), lambda i, j: (i, OFF_GATE_MLA // tn + j)),
            pl.BlockSpec((tm, tn), lambda i, j: (i, OFF_GATE_GDN // tn + j)),
            pl.BlockSpec((2, tn), lambda i, j: (0, j)),
        ],
        out_specs=pl.BlockSpec((tm, tn), lambda i, j: (i, j)),
        out_shape=jax.ShapeDtypeStruct((n, D_MODEL), BF16),
        compiler_params=_cparams(("parallel", "parallel")),
        name="merge",
    )(o_mla, o_gdn, w_o_mla, w_o_gdn, proj, proj, b_gate)


def _out_kernel(h_ref, w_ref, x_ref, o_ref):
    o_ref[...] = x_ref[...] + jnp.dot(h_ref[...], w_ref[...], preferred_element_type=F32)


def _out_proj(h, w_out, x2, tm=1024, tn=1024):
    n = h.shape[0]
    return pl.pallas_call(
        _out_kernel,
        grid=(n // tm, D_MODEL // tn),
        in_specs=[
            pl.BlockSpec((tm, D_MODEL), lambda i, j: (i, 0)),
            pl.BlockSpec((D_MODEL, tn), lambda i, j: (0, j)),
            pl.BlockSpec((tm, tn), lambda i, j: (i, j)),
        ],
        out_specs=pl.BlockSpec((tm, tn), lambda i, j: (i, j)),
        out_shape=jax.ShapeDtypeStruct((n, D_MODEL), F32),
        compiler_params=_cparams(("parallel", "parallel")),
        name="out_proj",
    )(h, w_out, x2)


def _pad_cols(a, width):
    return jnp.pad(a, ((0, 0), (0, width - a.shape[1])))


def kernel(x, positions, w_norm, w_in, b_gate, w_q_up, w_kv_up, g_q_latent, g_kv_latent, g_q_head, g_k_head,
           w_o_mla, conv_w, a_log, dt_bias, g_gdn_out, w_o_gdn, w_out):
    batch, seq, _ = x.shape
    n = batch * seq
    x2 = x.reshape(n, D_MODEL)

    o_cq, o_ckv, o_kr = 0, Q_LORA, Q_LORA + KV_LORA
    o_zm = o_kr + QK_ROPE
    o_qkv = o_zm + MLA_WIDTH
    o_b = o_qkv + GDN_QKV
    o_a = o_b + GDN_HEADS
    o_zg = o_a + GDN_HEADS
    o_gm = o_zg + GDN_WIDTH
    o_gg = o_gm + D_MODEL
    w_t = w_in.T
    rows = lambda lo, count: w_t[lo:lo + count]
    w_in_pt = jnp.concatenate([
        rows(o_gm, D_MODEL), rows(o_gg, D_MODEL), rows(o_qkv, GDN_QKV), rows(o_zm, MLA_WIDTH), rows(o_zg, GDN_WIDTH),
        rows(o_cq, Q_LORA), rows(o_ckv, KV_LORA), rows(o_kr, QK_ROPE),
        jnp.zeros((PROJ_WIDTH - OFF_KROPE - QK_ROPE, D_MODEL), w_in.dtype)], axis=0).astype(BF16)

    ngrp = GDN_HEADS // GDN_GROUP

    def gate_rows(beta_part, decay_part):
        width = beta_part.shape[1]
        out = []
        for g in range(ngrp):
            hs = slice(g * GDN_GROUP, (g + 1) * GDN_GROUP)
            out += [beta_part[hs], jnp.zeros((16 - GDN_GROUP, width), F32),
                    decay_part[hs], jnp.zeros((LANES - 16 - GDN_GROUP, width), F32)]
        return jnp.concatenate(out, axis=0)

    w_bd_t = gate_rows(rows(o_b, GDN_HEADS), rows(o_a, GDN_HEADS)).astype(BF16)
    zeros_h = jnp.zeros((GDN_HEADS, 1), F32)
    par = jnp.concatenate([gate_rows(zeros_h, a_log[:, None]).T, gate_rows(zeros_h, dt_bias[:, None]).T,
                           jnp.zeros((6, ngrp * LANES), F32)], axis=0)

    w_q = jnp.pad(w_q_up.reshape(Q_LORA, MLA_HEADS, QK_HEAD), ((0, 0), (0, 0), (0, QK_PAD - QK_HEAD)))
    w_q = w_q.reshape(Q_LORA, MLA_HEADS * QK_PAD).astype(BF16)
    w_kv3 = w_kv_up.reshape(KV_LORA, MLA_HEADS, QK_NOPE + V_HEAD)
    w_kn = w_kv3[:, :, :QK_NOPE].reshape(KV_LORA, MLA_HEADS * QK_NOPE).astype(BF16)
    w_vt = w_kv3[:, :, QK_NOPE:].reshape(KV_LORA, MLA_WIDTH).T.astype(BF16)
    g_qh = _pad_cols(g_q_head[None, :], QK_PAD)
    g_kn = g_k_head[None, :QK_NOPE]
    g_kr = _pad_cols(g_k_head[None, QK_NOPE:], LANES)
    inv_freq = ROPE_THETA ** (-jnp.arange(0, QK_ROPE, 2, dtype=F32) / QK_ROPE)
    invf = jnp.tile(inv_freq, LANES // (QK_ROPE // 2))[None, :]
    pos2 = positions.reshape(n, 1)

    xn, gb = _norm_gate(x2, w_norm[None, :], w_bd_t, par)
    proj = _matmul_nt(xn, w_in_pt, BF16, 1024, 1024, "in_proj")

    q, k, vt = _mla_prep(proj, pos2, invf, g_q_latent[None, :], g_kv_latent[None, :], w_q, w_kn, w_vt, g_qh, g_kn,
                         g_kr, batch, seq)
    o_mla = _attention(q, k, vt, proj, batch, seq)

    u, w, qd, ai, kdt, gl8 = _gdn_prep(proj, conv_w, gb, batch, seq)
    o_gdn = _gdn_scan(u, w, qd, ai, kdt, gl8, proj, g_gdn_out[None, :], batch, seq)

    h = _merge(o_mla, o_gdn, w_o_mla.astype(BF16), w_o_gdn.astype(BF16), proj, b_gate)
    out = _out_proj(h, w_out.astype(BF16), x2)
    return out.reshape(batch, seq, D_MODEL)
```

```python
import functools

import jax
import jax.numpy as jnp
from jax import lax
from jax.experimental import pallas as pl
from jax.experimental.pallas import tpu as pltpu

F32 = jnp.float32
BF16 = jnp.bfloat16

D_MODEL = 4096
CHUNK = 64
RMS_EPS = 1e-6
L2_EPS = 1e-6

MLA_HEADS = 16
QK_NOPE = 128
QK_ROPE = 64
QK_HEAD = QK_NOPE + QK_ROPE
V_HEAD = 128
Q_LORA = 1024
KV_LORA = 512
ROPE_THETA = 10000.0
MLA_WIDTH = MLA_HEADS * V_HEAD

GDN_HEADS = 16
GDN_DK = 128
GDN_DV = 128
GDN_KEY_WIDTH = GDN_HEADS * GDN_DK
GDN_WIDTH = GDN_HEADS * GDN_DV
GDN_QKV = 2 * GDN_KEY_WIDTH + GDN_WIDTH
CONV_K = 4

LANES = 128
QK_PAD = 256
GDN_BLOCK = 128
GDN_GROUP = 4

OFF_GATE_MLA = 0
OFF_GATE_GDN = OFF_GATE_MLA + D_MODEL
OFF_QKV = OFF_GATE_GDN + D_MODEL
OFF_Z_MLA = OFF_QKV + GDN_QKV
OFF_Z_GDN = OFF_Z_MLA + MLA_WIDTH
OFF_CQ = OFF_Z_GDN + GDN_WIDTH
OFF_CKV = OFF_CQ + Q_LORA
OFF_KROPE = OFF_CKV + KV_LORA
PROJ_WIDTH = 20480

VMEM_LIMIT = 56 * 1024 * 1024
LOG2E = 1.4426950408889634
LOOP_TILES = 4


def _cparams(sem):
    return pltpu.CompilerParams(dimension_semantics=sem, vmem_limit_bytes=VMEM_LIMIT)


def _sigmoid(x):
    return 1.0 / (1.0 + jnp.exp(-x))


def _norm_gate_kernel(x_ref, wn_ref, wbd_ref, par_ref, xn_ref, gb_ref):
    x = x_ref[...]
    ms = jnp.mean(x * x, axis=-1, keepdims=True)
    xn = (x * lax.rsqrt(ms + RMS_EPS) * wn_ref[...]).astype(BF16)
    xn_ref[...] = xn
    bd = lax.dot_general(xn, wbd_ref[...], (((1,), (1,)), ((), ())), preferred_element_type=F32)
    neg_a = -jnp.exp(par_ref[0:1, :])
    z = bd + par_ref[1:2, :]
    softplus = jnp.maximum(z, 0.0) + jnp.log1p(jnp.exp(-jnp.abs(z)))
    lane = lax.broadcasted_iota(jnp.int32, bd.shape, 1) % LANES
    gb_ref[...] = jnp.where(lane < 16, _sigmoid(bd), neg_a * softplus)


def _norm_gate(x2, w_norm, w_bd_t, par, tm=256):
    n = x2.shape[0]
    gw = w_bd_t.shape[0]
    return pl.pallas_call(
        _norm_gate_kernel,
        grid=(n // tm,),
        in_specs=[
            pl.BlockSpec((tm, D_MODEL), lambda i: (i, 0)),
            pl.BlockSpec((1, D_MODEL), lambda i: (0, 0)),
            pl.BlockSpec((gw, D_MODEL), lambda i: (0, 0)),
            pl.BlockSpec((8, gw), lambda i: (0, 0)),
        ],
        out_specs=[
            pl.BlockSpec((tm, D_MODEL), lambda i: (i, 0)),
            pl.BlockSpec((tm, gw), lambda i: (i, 0)),
        ],
        out_shape=[
            jax.ShapeDtypeStruct((n, D_MODEL), BF16),
            jax.ShapeDtypeStruct((n, gw), F32),
        ],
        compiler_params=_cparams(("parallel",)),
        name="norm_gate",
    )(x2, w_norm, w_bd_t, par)


def _matmul_nt_kernel(a_ref, bt_ref, o_ref):
    o_ref[...] = lax.dot_general(a_ref[...], bt_ref[...], (((1,), (1,)), ((), ())),
                                 preferred_element_type=F32).astype(o_ref.dtype)


def _matmul_nt(a, b_t, out_dtype, tm, tn, name):
    m, k = a.shape
    n = b_t.shape[0]
    return pl.pallas_call(
        _matmul_nt_kernel,
        grid=(m // tm, n // tn),
        in_specs=[
            pl.BlockSpec((tm, k), lambda i, j: (i, 0)),
            pl.BlockSpec((tn, k), lambda i, j: (j, 0)),
        ],
        out_specs=pl.BlockSpec((tm, tn), lambda i, j: (i, j)),
        out_shape=jax.ShapeDtypeStruct((m, n), out_dtype),
        compiler_params=_cparams(("parallel", "parallel")),
        name=name,
    )(a, b_t)


def _rope_tables(pos_ref, invf_ref):
    ang = pos_ref[...].astype(F32) * invf_ref[...]
    cos = jnp.cos(ang)
    sin = jnp.sin(ang)
    lane = lax.broadcasted_iota(jnp.int32, ang.shape, 1)
    c = jnp.where(lane < QK_ROPE, cos, 0.0)
    s_lo = jnp.where(lane < QK_ROPE // 2, -sin, 0.0)
    s_hi = jnp.where((lane >= QK_ROPE // 2) & (lane < QK_ROPE), sin, 0.0)
    return c, s_lo, s_hi


def _rope(r, tables):
    c, s_lo, s_hi = tables
    half = QK_ROPE // 2
    return r * c + pltpu.roll(r, LANES - half, 1) * s_lo + pltpu.roll(r, half, 1) * s_hi


def _rms_rows(c, gain):
    ms = jnp.mean(c * c, axis=-1, keepdims=True)
    return c * lax.rsqrt(ms + RMS_EPS) * gain


def _mla_prep_kernel(cq_ref, ckv_ref, kr_ref, pos_ref, invf_ref, glq_ref, glkv_ref, wq_ref, wkn_ref, wvt_ref,
                     gqh_ref, gkn_ref, gkr_ref, q_ref, k_ref, vt_ref):
    heads = range(MLA_HEADS)
    tables = _rope_tables(pos_ref, invf_ref)
    cq = _rms_rows(cq_ref[...].astype(F32), glq_ref[...]).astype(BF16)
    ckv = _rms_rows(ckv_ref[...].astype(F32), glkv_ref[...]).astype(BF16)
    q = jnp.dot(cq, wq_ref[...], preferred_element_type=F32)
    k_nope = jnp.dot(ckv, wkn_ref[...], preferred_element_type=F32)
    vt_ref[...] = lax.dot_general(wvt_ref[...], ckv, (((1,), (1,)), ((), ())),
                                  preferred_element_type=F32).astype(BF16)
    kr = kr_ref[...].astype(F32)
    ss_r = jnp.sum(kr * kr, axis=-1, keepdims=True)
    k_rot = _rope(kr * gkr_ref[...], tables)
    g_qn = gqh_ref[:, :QK_NOPE]
    g_qr = gqh_ref[:, QK_NOPE:]
    g_kn = gkn_ref[...]
    scale = QK_HEAD ** -0.5 * LOG2E

    qa = [q[:, h * QK_PAD:h * QK_PAD + QK_NOPE] for h in heads]
    qr = [q[:, h * QK_PAD + QK_NOPE:(h + 1) * QK_PAD] for h in heads]
    kn = [k_nope[:, h * QK_NOPE:(h + 1) * QK_NOPE] for h in heads]
    ss_q = [jnp.sum(a * a + r * r, axis=-1, keepdims=True) for a, r in zip(qa, qr)]
    ss_k = [jnp.sum(x * x, axis=-1, keepdims=True) + ss_r for x in kn]
    inv_q = [lax.rsqrt(s * (1.0 / QK_HEAD) + RMS_EPS) * scale for s in ss_q]
    inv_k = [lax.rsqrt(s * (1.0 / QK_HEAD) + RMS_EPS) for s in ss_k]
    q_rot = [_rope(r * g_qr, tables) for r in qr]
    for h in heads:
        q_ref[:, h * QK_PAD:h * QK_PAD + QK_NOPE] = (qa[h] * inv_q[h] * g_qn).astype(BF16)
        q_ref[:, h * QK_PAD + QK_NOPE:(h + 1) * QK_PAD] = (q_rot[h] * inv_q[h]).astype(BF16)
        k_ref[:, h * QK_PAD:h * QK_PAD + QK_NOPE] = (kn[h] * inv_k[h] * g_kn).astype(BF16)
        k_ref[:, h * QK_PAD + QK_NOPE:(h + 1) * QK_PAD] = (k_rot * inv_k[h]).astype(BF16)


def _mla_prep(proj, pos2, invf, g_q_lat, g_kv_lat, w_q, w_kn, w_vt, g_qh, g_kn, g_kr, batch, seq, tm=256):
    n = proj.shape[0]
    width = MLA_HEADS * QK_PAD
    ns = seq // tm
    tok = lambda col: (lambda b, i: (b * ns + i, col))
    const = lambda b, i: (0, 0)
    return pl.pallas_call(
        _mla_prep_kernel,
        grid=(batch, ns),
        in_specs=[
            pl.BlockSpec((tm, Q_LORA), tok(OFF_CQ // Q_LORA)),
            pl.BlockSpec((tm, KV_LORA), tok(OFF_CKV // KV_LORA)),
            pl.BlockSpec((tm, LANES), tok(OFF_KROPE // LANES)),
            pl.BlockSpec((tm, 1), tok(0)),
            pl.BlockSpec((1, LANES), const),
            pl.BlockSpec((1, Q_LORA), const),
            pl.BlockSpec((1, KV_LORA), const),
            pl.BlockSpec((Q_LORA, width), const),
            pl.BlockSpec((KV_LORA, MLA_HEADS * QK_NOPE), const),
            pl.BlockSpec((MLA_WIDTH, KV_LORA), const),
            pl.BlockSpec((1, QK_PAD), const),
            pl.BlockSpec((1, LANES), const),
            pl.BlockSpec((1, LANES), const),
        ],
        out_specs=[
            pl.BlockSpec((tm, width), tok(0)),
            pl.BlockSpec((tm, width), tok(0)),
            pl.BlockSpec((MLA_WIDTH, tm), lambda b, i: (b, i)),
        ],
        out_shape=[
            jax.ShapeDtypeStruct((n, width), BF16),
            jax.ShapeDtypeStruct((n, width), BF16),
            jax.ShapeDtypeStruct((batch * MLA_WIDTH, seq), BF16),
        ],
        compiler_params=_cparams(("parallel", "parallel")),
        name="mla_prep",
    )(proj, proj, proj, pos2, invf, g_q_lat, g_kv_lat, w_q, w_kn, w_vt, g_qh, g_kn, g_kr)


def _attn_kernel(q_ref, k_ref, vt_ref, z_ref, o_ref, *, seq, tq):
    nq = seq // tq
    hq = tq // 2

    def chunk_mask(n_keys, query_offset):
        key_chunk = lax.broadcasted_iota(jnp.int32, (n_keys, hq), 0) // CHUNK
        qry_chunk = (lax.broadcasted_iota(jnp.int32, (n_keys, hq), 1) + query_offset) // CHUNK
        return key_chunk <= qry_chunk

    diag_mask_x = chunk_mask(hq, 0)
    diag_mask_y = chunk_mask(tq, hq)

    def qk(q, k):
        return lax.dot_general(k, q, (((1,), (1,)), ((), ())), preferred_element_type=F32)

    def softmax_step(st, carry, mask):
        m, l, acc = carry
        if mask is not None:
            st = jnp.where(mask, st, -jnp.inf)
        m_new = jnp.maximum(m, jnp.max(st, axis=0, keepdims=True))
        alpha = jnp.exp2(m - m_new)
        p = jnp.exp2(st - m_new)
        l = alpha * l + jnp.sum(p, axis=0, keepdims=True)
        return p.astype(BF16), alpha, (m_new, l, acc)

    def pv(vt, p, alpha, carry):
        m, l, acc = carry
        return m, l, alpha * acc + jnp.dot(vt, p, preferred_element_type=F32)

    def run_tiles(qs, tiles, carries):
        carries = list(carries)
        n = len(tiles)
        scores = [None] * n
        probs = [None] * n

        def emit_qk(t):
            scores[t] = [qk(qs[x], tiles[t][x][0]) for x in range(2)]

        def emit_softmax(t):
            out = []
            for x in range(2):
                p, alpha, carries[x] = softmax_step(scores[t][x], carries[x], tiles[t][x][2])
                out.append((p, alpha))
            probs[t] = out

        def emit_pv(t):
            for x in range(2):
                p, alpha = probs[t][x]
                carries[x] = pv(tiles[t][x][1], p, alpha, carries[x])

        emit_qk(0)
        for t in range(n):
            if t + 1 < n:
                emit_qk(t + 1)
            emit_softmax(t)
            if t > 0:
                emit_pv(t - 1)
        emit_pv(n - 1)
        return tuple(carries)

    def full_tile(off):
        half = (k_ref[pl.ds(off, tq), :], vt_ref[:, pl.ds(off, tq)], None)
        return half, half

    def diag_tile(off):
        return ((k_ref[pl.ds(off, hq), :], vt_ref[:, pl.ds(off, hq)], diag_mask_x),
                (k_ref[pl.ds(off, tq), :], vt_ref[:, pl.ds(off, tq)], diag_mask_y))

    for i in range(nq):
        qs = (q_ref[i * tq:i * tq + hq, :], q_ref[i * tq + hq:(i + 1) * tq, :])

        def body(t, carries, qs=qs):
            off = pl.multiple_of(t * (LOOP_TILES * tq), LOOP_TILES * tq)
            return run_tiles(qs, [full_tile(off + s * tq) for s in range(LOOP_TILES)], carries)

        init = (jnp.full((1, hq), -1e30, F32), jnp.zeros((1, hq), F32), jnp.zeros((V_HEAD, hq), F32))
        carries = (init, init)
        n_loop = i // LOOP_TILES
        if n_loop > 0:
            carries = lax.fori_loop(0, n_loop, body, carries)
        tail = [full_tile(s * tq) for s in range(n_loop * LOOP_TILES, i)] + [diag_tile(i * tq)]
        carries = run_tiles(qs, tail, carries)
        for x in range(2):
            rows = slice(i * tq + x * hq, i * tq + (x + 1) * hq)
            _, l, acc = carries[x]
            z = z_ref[rows, :].astype(F32)
            o_ref[rows, :] = ((acc / l).T * (z * _sigmoid(z))).astype(o_ref.dtype)


def _attention(q, k, vt, proj, batch, seq, tq=512):
    n = q.shape[0]
    return pl.pallas_call(
        functools.partial(_attn_kernel, seq=seq, tq=tq),
        grid=(batch, MLA_HEADS),
        in_specs=[
            pl.BlockSpec((seq, QK_PAD), lambda b, h: (b, h)),
            pl.BlockSpec((seq, QK_PAD), lambda b, h: (b, h)),
            pl.BlockSpec((V_HEAD, seq), lambda b, h: (b * MLA_HEADS + h, 0)),
            pl.BlockSpec((seq, V_HEAD), lambda b, h: (b, OFF_Z_MLA // V_HEAD + h)),
        ],
        out_specs=pl.BlockSpec((seq, V_HEAD), lambda b, h: (b, h)),
        out_shape=jax.ShapeDtypeStruct((n, MLA_WIDTH), BF16),
        compiler_params=_cparams(("parallel", "parallel")),
        name="mla_attention",
    )(q, k, vt, proj)


def _gdn_prep_kernel(q_ref, k_ref, v_ref, qh_ref, kh_ref, vh_ref, cwq_ref, cwk_ref, cwv_ref, gb_ref,
                     u_ref, w_ref, qd_ref, ai_ref, kdt_ref, gl_ref, *, tt):
    first = pl.program_id(1) == 0
    blk = GDN_BLOCK
    nb = tt // blk
    pair = 2 * LANES
    halo_rows = 8

    gb = gb_ref[...]
    rmod = lax.broadcasted_iota(jnp.int32, (tt, LANES), 0) % blk
    gc = gb
    s = 1
    while s < blk:
        gc = gc + jnp.where(rmod >= s, pltpu.roll(gc, s, 0), 0.0)
        s *= 2
    gc_t = [gc[p * blk:(p + 1) * blk].T for p in range(nb)]

    def conv_silu(x_ref, h_ref, cw_ref):
        halo = h_ref[...].astype(F32)[16 - halo_rows:16]
        xe = jnp.concatenate([jnp.where(first, 0.0, halo), x_ref[...].astype(F32)], axis=0)
        cw = cw_ref[...]
        y = xe * cw[CONV_K - 1:CONV_K]
        for sh in range(1, CONV_K):
            y = y + pltpu.roll(xe, sh, 0) * cw[CONV_K - 1 - sh:CONV_K - sh]
        y = y[halo_rows:]
        return y * _sigmoid(y)

    def l2norm_heads(x, scale):
        parts = []
        for hh in range(GDN_GROUP):
            xs = x[:, hh * LANES:(hh + 1) * LANES]
            parts.append(xs * (lax.rsqrt(jnp.sum(xs * xs, axis=-1, keepdims=True) + L2_EPS) * scale))
        return jnp.concatenate(parts, axis=1)

    def lane_bcast(x, lane0):
        return jnp.concatenate(
            [jnp.broadcast_to(x[:, lane0 + hh:lane0 + hh + 1], (tt, LANES)) for hh in range(GDN_GROUP)], axis=1)

    q = l2norm_heads(conv_silu(q_ref, qh_ref, cwq_ref), GDN_DK ** -0.5)
    k = l2norm_heads(conv_silu(k_ref, kh_ref, cwk_ref), 1.0)
    v = conv_silu(v_ref, vh_ref, cwv_ref)
    beta_b = lane_bcast(gb, 0)
    gcol_b = lane_bcast(gc, 16)
    egc = jnp.exp(gcol_b)
    kb = k * beta_b
    vb = v * beta_b
    kbe = kb * egc
    qd_ref[...] = (q * egc).astype(BF16)
    glast_b = jnp.concatenate(
        [jnp.broadcast_to(gcol_b[(p + 1) * blk - 1:(p + 1) * blk, :], (blk, gcol_b.shape[1])) for p in range(nb)], axis=0)
    kd = k * jnp.exp(glast_b - gcol_b)
    for p in range(nb):
        r = slice(p * blk, (p + 1) * blk)
        gl_ref[p * 8:(p + 1) * 8, :] = jnp.exp(glast_b[p * blk:p * blk + 8, :])
        for hh in range(GDN_GROUP):
            c = slice(hh * LANES, (hh + 1) * LANES)
            kdt_ref[c, r] = kd[r, c].T.astype(BF16)

    ri = lax.broadcasted_iota(jnp.int32, (blk, pair), 0)
    ci = lax.broadcasted_iota(jnp.int32, (blk, pair), 1) % LANES
    causal = ri >= ci
    strict = ri > ci
    zero = jnp.zeros((blk, LANES), BF16)

    def bdiag(x):
        return jnp.concatenate([jnp.concatenate([x[:, :LANES], zero], axis=1),
                                jnp.concatenate([zero, x[:, LANES:]], axis=1)], axis=0)

    def mm(a, b):
        return jnp.dot(a, b, preferred_element_type=F32)

    items = [(p, j) for p in range(nb) for j in range(GDN_GROUP // 2)]
    rc = lambda p, j: (slice(p * blk, (p + 1) * blk), slice(j * pair, (j + 1) * pair))

    kq = []
    for p, j in items:
        r, c = rc(p, j)
        lhs = jnp.concatenate([kb[r, c].astype(BF16), q[r, c].astype(BF16)], axis=0)
        kq.append(lax.dot_general(lhs, bdiag(k[r, c].astype(BF16)), (((1,), (1,)), ((), ())),
                                  preferred_element_type=F32))
    e = []
    for (p, j), kq_i in zip(items, kq):
        r, c = rc(p, j)
        grow = jnp.concatenate([gc_t[p][16 + 2 * j:17 + 2 * j, :], gc_t[p][17 + 2 * j:18 + 2 * j, :]], axis=1)
        diff = gcol_b[r, c] - grow
        decay = jnp.where(causal, jnp.exp(jnp.where(causal, diff, 0.0)), 0.0)
        ai_ref[r, c] = (kq_i[blk:] * decay).astype(BF16)
        e.append(jnp.where(strict, -(kq_i[:blk] * decay), 0.0))
    eb = [x.astype(BF16) for x in e]
    pw = [mm(x, bdiag(x)) for x in eb]
    span = 2
    while span * 2 < blk:
        pwb = [x.astype(BF16) for x in pw]
        both = [mm(jnp.concatenate([e_i.astype(BF16), p_i], axis=0), bdiag(p_i)) for e_i, p_i in zip(e, pwb)]
        e = [e_i + p_i + b_i[:blk] for e_i, p_i, b_i in zip(e, pw, both)]
        pw = [b_i[blk:] for b_i in both]
        span *= 2
    last = [mm(e_i.astype(BF16), bdiag(p_i.astype(BF16))) for e_i, p_i in zip(e, pw)]
    e = [e_i + p_i + l_i for e_i, p_i, l_i in zip(e, pw, last)]
    for (p, j), e_i in zip(items, e):
        r, _ = rc(p, j)
        for s in range(2):
            c = slice((2 * j + s) * LANES, (2 * j + s + 1) * LANES)
            rhs = jnp.concatenate([vb[r, c].astype(BF16), kbe[r, c].astype(BF16)], axis=1)
            sol = mm(e_i[:, s * LANES:(s + 1) * LANES].astype(BF16), rhs)
            u_ref[r, c] = vb[r, c] + sol[:, :LANES]
            w_ref[r, c] = (kbe[r, c] + sol[:, LANES:]).astype(BF16)


def _gdn_prep(proj, conv_w, gb, batch, seq, tt=512):
    n = proj.shape[0]
    gw = GDN_GROUP * LANES
    ns = seq // tt
    ngrp = GDN_HEADS // GDN_GROUP
    col = lambda base: (lambda b, i, g: (b * ns + i, base // gw + g))
    halo = lambda base: (lambda b, i, g: (jnp.maximum((b * ns + i) * (tt // 16) - 1, 0), base // gw + g))
    cwc = lambda base: (lambda b, i, g: (0, base // gw + g))
    tok = lambda b, i, g: (b * ns + i, g)
    return pl.pallas_call(
        functools.partial(_gdn_prep_kernel, tt=tt),
        grid=(batch, ns, ngrp),
        in_specs=[
            pl.BlockSpec((tt, gw), col(OFF_QKV)),
            pl.BlockSpec((tt, gw), col(OFF_QKV + GDN_KEY_WIDTH)),
            pl.BlockSpec((tt, gw), col(OFF_QKV + 2 * GDN_KEY_WIDTH)),
            pl.BlockSpec((16, gw), halo(OFF_QKV)),
            pl.BlockSpec((16, gw), halo(OFF_QKV + GDN_KEY_WIDTH)),
            pl.BlockSpec((16, gw), halo(OFF_QKV + 2 * GDN_KEY_WIDTH)),
            pl.BlockSpec((CONV_K, gw), cwc(0)),
            pl.BlockSpec((CONV_K, gw), cwc(GDN_KEY_WIDTH)),
            pl.BlockSpec((CONV_K, gw), cwc(2 * GDN_KEY_WIDTH)),
            pl.BlockSpec((tt, LANES), tok),
        ],
        out_specs=[
            pl.BlockSpec((tt, gw), tok),
            pl.BlockSpec((tt, gw), tok),
            pl.BlockSpec((tt, gw), tok),
            pl.BlockSpec((tt, gw), tok),
            pl.BlockSpec((gw, tt), lambda b, i, g: (b * ngrp + g, i)),
            pl.BlockSpec((tt // GDN_BLOCK * 8, gw), tok),
        ],
        out_shape=[
            jax.ShapeDtypeStruct((n, GDN_WIDTH), F32),
            jax.ShapeDtypeStruct((n, GDN_WIDTH), BF16),
            jax.ShapeDtypeStruct((n, GDN_WIDTH), BF16),
            jax.ShapeDtypeStruct((n, GDN_WIDTH), BF16),
            jax.ShapeDtypeStruct((batch * GDN_HEADS * GDN_DK, seq), BF16),
            jax.ShapeDtypeStruct((n // GDN_BLOCK * 8, GDN_WIDTH), F32),
        ],
        compiler_params=_cparams(("parallel", "parallel", "parallel")),
        name="gdn_prep",
    )(proj, proj, proj, proj, proj, proj, conv_w, conv_w, conv_w, gb)


def _gdn_scan_kernel(u_ref, w_ref, qd_ref, ai_ref, kdt_ref, gl_ref, z_ref, g_ref, o_ref, s_ref, *, ts, hg):
    @pl.when(pl.program_id(2) == 0)
    def _():
        s_ref[...] = jnp.zeros_like(s_ref)

    blk = GDN_BLOCK
    gain = g_ref[...]
    for p in range(ts // blk):
        r = slice(p * blk, (p + 1) * blk)
        for hh in range(hg):
            c = slice(hh * LANES, (hh + 1) * LANES)
            state = s_ref[hh]
            lhs1 = jnp.concatenate([w_ref[r, c], qd_ref[r, c]], axis=0)
            r1 = jnp.dot(lhs1, state.astype(BF16), preferred_element_type=F32)
            v_new = u_ref[r, c] - r1[:blk]
            lhs2 = jnp.concatenate([ai_ref[r, c], kdt_ref[c, r]], axis=0)
            r2 = jnp.dot(lhs2, v_new.astype(BF16), preferred_element_type=F32)
            o = r1[blk:] + r2[:blk]
            gl = jnp.concatenate([gl_ref[p * 8:(p + 1) * 8, c]] * (blk // 8), axis=0)
            s_ref[hh] = state * gl + r2[blk:]
            z = z_ref[r, c].astype(F32)
            o_ref[r, c] = (_rms_rows(o, gain) * (z * _sigmoid(z))).astype(o_ref.dtype)


def _gdn_scan(u, w, qd, ai, kdt, gl8, proj, g_out, batch, seq, ts=512, hg=8):
    n = u.shape[0]
    gw = hg * LANES
    ns = seq // ts
    ngrp = GDN_HEADS // hg
    tok = lambda b, g, i: (b * ns + i, g)
    return pl.pallas_call(
        functools.partial(_gdn_scan_kernel, ts=ts, hg=hg),
        grid=(batch, ngrp, ns),
        in_specs=[
            pl.BlockSpec((ts, gw), tok),
            pl.BlockSpec((ts, gw), tok),
            pl.BlockSpec((ts, gw), tok),
            pl.BlockSpec((ts, gw), tok),
            pl.BlockSpec((gw, ts), lambda b, g, i: (b * ngrp + g, i)),
            pl.BlockSpec((ts // GDN_BLOCK * 8, gw), tok),
            pl.BlockSpec((ts, gw), lambda b, g, i: (b * ns + i, OFF_Z_GDN // gw + g)),
            pl.BlockSpec((1, LANES), lambda b, g, i: (0, 0)),
        ],
        out_specs=pl.BlockSpec((ts, gw), tok),
        out_shape=jax.ShapeDtypeStruct((n, GDN_WIDTH), BF16),
        scratch_shapes=[pltpu.VMEM((hg, GDN_DK, GDN_DV), F32)],
        compiler_params=_cparams(("parallel", "parallel", "arbitrary")),
        name="gdn_scan",
    )(u, w, qd, ai, kdt, gl8, proj, g_out)


def _merge_kernel(a1_ref, a2_ref, w1_ref, w2_ref, g1_ref, g2_ref, b_ref, h_ref):
    y1 = jnp.dot(a1_ref[...], w1_ref[...], preferred_element_type=F32)
    y2 = jnp.dot(a2_ref[...], w2_ref[...], preferred_element_type=F32)
    s1 = _sigmoid(g1_ref[...].astype(F32) + b_ref[0:1, :])
    s2 = _sigmoid(g2_ref[...].astype(F32) + b_ref[1:2, :])
    h_ref[...] = (s1 * y1 + s2 * y2).astype(h_ref.dtype)


def _merge(o_mla, o_gdn, w_o_mla, w_o_gdn, proj, b_gate, tm=1024, tn=1024):
    n = o_mla.shape[0]
    return pl.pallas_call(
        _merge_kernel,
        grid=(n // tm, D_MODEL // tn),
        in_specs=[
            pl.BlockSpec((tm, MLA_WIDTH), lambda i, j: (i, 0)),
            pl.BlockSpec((tm, GDN_WIDTH), lambda i, j: (i, 0)),
            pl.BlockSpec((MLA_WIDTH, tn), lambda i, j: (0, j)),
            pl.BlockSpec((GDN_WIDTH, tn), lambda i, j: (0, j)),
            pl.BlockSpec((tm, tn), lambda i, j: (i, OFF_GATE_MLA // tn + j)),
            pl.BlockSpec((tm, tn), lambda i, j: (i, OFF_GATE_GDN // tn + j)),
            pl.BlockSpec((2, tn), lambda i, j: (0, j)),
        ],
        out_specs=pl.BlockSpec((tm, tn), lambda i, j: (i, j)),
        out_shape=jax.ShapeDtypeStruct((n, D_MODEL), BF16),
        compiler_params=_cparams(("parallel", "parallel")),
        name="merge",
    )(o_mla, o_gdn, w_o_mla, w_o_gdn, proj, proj, b_gate)


def _out_kernel(h_ref, w_ref, x_ref, o_ref):
    o_ref[...] = x_ref[...] + jnp.dot(h_ref[...], w_ref[...], preferred_element_type=F32)


def _out_proj(h, w_out, x2, tm=1024, tn=1024):
    n = h.shape[0]
    return pl.pallas_call(
        _out_kernel,
        grid=(n // tm, D_MODEL // tn),
        in_specs=[
            pl.BlockSpec((tm, D_MODEL), lambda i, j: (i, 0)),
            pl.BlockSpec((D_MODEL, tn), lambda i, j: (0, j)),
            pl.BlockSpec((tm, tn), lambda i, j: (i, j)),
        ],
        out_specs=pl.BlockSpec((tm, tn), lambda i, j: (i, j)),
        out_shape=jax.ShapeDtypeStruct((n, D_MODEL), F32),
        compiler_params=_cparams(("parallel", "parallel")),
        name="out_proj",
    )(h, w_out, x2)


def _pad_cols(a, width):
    return jnp.pad(a, ((0, 0), (0, width - a.shape[1])))


def kernel(x, positions, w_norm, w_in, b_gate, w_q_up, w_kv_up, g_q_latent, g_kv_latent, g_q_head, g_k_head,
           w_o_mla, conv_w, a_log, dt_bias, g_gdn_out, w_o_gdn, w_out):
    batch, seq, _ = x.shape
    n = batch * seq
    x2 = x.reshape(n, D_MODEL)

    o_cq, o_ckv, o_kr = 0, Q_LORA, Q_LORA + KV_LORA
    o_zm = o_kr + QK_ROPE
    o_qkv = o_zm + MLA_WIDTH
    o_b = o_qkv + GDN_QKV
    o_a = o_b + GDN_HEADS
    o_zg = o_a + GDN_HEADS
    o_gm = o_zg + GDN_WIDTH
    o_gg = o_gm + D_MODEL
    w_t = w_in.T
    rows = lambda lo, count: w_t[lo:lo + count]
    w_in_pt = jnp.concatenate([
        rows(o_gm, D_MODEL), rows(o_gg, D_MODEL), rows(o_qkv, GDN_QKV), rows(o_zm, MLA_WIDTH), rows(o_zg, GDN_WIDTH),
        rows(o_cq, Q_LORA), rows(o_ckv, KV_LORA), rows(o_kr, QK_ROPE),
        jnp.zeros((PROJ_WIDTH - OFF_KROPE - QK_ROPE, D_MODEL), w_in.dtype)], axis=0).astype(BF16)

    ngrp = GDN_HEADS // GDN_GROUP

    def gate_rows(beta_part, decay_part):
        width = beta_part.shape[1]
        out = []
        for g in range(ngrp):
            hs = slice(g * GDN_GROUP, (g + 1) * GDN_GROUP)
            out += [beta_part[hs], jnp.zeros((16 - GDN_GROUP, width), F32),
                    decay_part[hs], jnp.zeros((LANES - 16 - GDN_GROUP, width), F32)]
        return jnp.concatenate(out, axis=0)

    w_bd_t = gate_rows(rows(o_b, GDN_HEADS), rows(o_a, GDN_HEADS)).astype(BF16)
    zeros_h = jnp.zeros((GDN_HEADS, 1), F32)
    par = jnp.concatenate([gate_rows(zeros_h, a_log[:, None]).T, gate_rows(zeros_h, dt_bias[:, None]).T,
                           jnp.zeros((6, ngrp * LANES), F32)], axis=0)

    w_q = jnp.pad(w_q_up.reshape(Q_LORA, MLA_HEADS, QK_HEAD), ((0, 0), (0, 0), (0, QK_PAD - QK_HEAD)))
    w_q = w_q.reshape(Q_LORA, MLA_HEADS * QK_PAD).astype(BF16)
    w_kv3 = w_kv_up.reshape(KV_LORA, MLA_HEADS, QK_NOPE + V_HEAD)
    w_kn = w_kv3[:, :, :QK_NOPE].reshape(KV_LORA, MLA_HEADS * QK_NOPE).astype(BF16)
    w_vt = w_kv3[:, :, QK_NOPE:].reshape(KV_LORA, MLA_WIDTH).T.astype(BF16)
    g_qh = _pad_cols(g_q_head[None, :], QK_PAD)
    g_kn = g_k_head[None, :QK_NOPE]
    g_kr = _pad_cols(g_k_head[None, QK_NOPE:], LANES)
    inv_freq = ROPE_THETA ** (-jnp.arange(0, QK_ROPE, 2, dtype=F32) / QK_ROPE)
    invf = jnp.tile(inv_freq, LANES // (QK_ROPE // 2))[None, :]
    pos2 = positions.reshape(n, 1)

    xn, gb = _norm_gate(x2, w_norm[None, :], w_bd_t, par)
    proj = _matmul_nt(xn, w_in_pt, BF16, 1024, 1024, "in_proj")

    q, k, vt = _mla_prep(proj, pos2, invf, g_q_latent[None, :], g_kv_latent[None, :], w_q, w_kn, w_vt, g_qh, g_kn,
                         g_kr, batch, seq)
    o_mla = _attention(q, k, vt, proj, batch, seq)

    u, w, qd, ai, kdt, gl8 = _gdn_prep(proj, conv_w, gb, batch, seq)
    o_gdn = _gdn_scan(u, w, qd, ai, kdt, gl8, proj, g_gdn_out[None, :], batch, seq)

    h = _merge(o_mla, o_gdn, w_o_mla.astype(BF16), w_o_gdn.astype(BF16), proj, b_gate)
    out = _out_proj(h, w_out.astype(BF16), x2)
    return out.reshape(batch, seq, D_MODEL)
```

```python
import functools

import jax
import jax.numpy as jnp
from jax import lax
from jax.experimental import pallas as pl
from jax.experimental.pallas import tpu as pltpu

F32 = jnp.float32
BF16 = jnp.bfloat16

D_MODEL = 4096
CHUNK = 64
RMS_EPS = 1e-6
L2_EPS = 1e-6

MLA_HEADS = 16
QK_NOPE = 128
QK_ROPE = 64
QK_HEAD = QK_NOPE + QK_ROPE
V_HEAD = 128
Q_LORA = 1024
KV_LORA = 512
ROPE_THETA = 10000.0
MLA_WIDTH = MLA_HEADS * V_HEAD

GDN_HEADS = 16
GDN_DK = 128
GDN_DV = 128
GDN_KEY_WIDTH = GDN_HEADS * GDN_DK
GDN_WIDTH = GDN_HEADS * GDN_DV
GDN_QKV = 2 * GDN_KEY_WIDTH + GDN_WIDTH
CONV_K = 4

LANES = 128
QK_PAD = 256
GDN_BLOCK = 128
GDN_GROUP = 4

OFF_GATE_MLA = 0
OFF_GATE_GDN = OFF_GATE_MLA + D_MODEL
OFF_QKV = OFF_GATE_GDN + D_MODEL
OFF_Z_MLA = OFF_QKV + GDN_QKV
OFF_Z_GDN = OFF_Z_MLA + MLA_WIDTH
OFF_CQ = OFF_Z_GDN + GDN_WIDTH
OFF_CKV = OFF_CQ + Q_LORA
OFF_KROPE = OFF_CKV + KV_LORA
PROJ_WIDTH = 20480

VMEM_LIMIT = 56 * 1024 * 1024
LOG2E = 1.4426950408889634
VT_ROWS = 144


def _cparams(sem):
    return pltpu.CompilerParams(dimension_semantics=sem, vmem_limit_bytes=VMEM_LIMIT)


def _sigmoid(x):
    return 1.0 / (1.0 + jnp.exp(-x))


def _silu(x):
    h = 0.5 * x
    return h + h * jnp.tanh(h)


def _norm_gate_kernel(x_ref, wn_ref, wbd_ref, par_ref, xn_ref, gb_ref):
    x = x_ref[...]
    ms = jnp.mean(x * x, axis=-1, keepdims=True)
    xn = (x * lax.rsqrt(ms + RMS_EPS) * wn_ref[...]).astype(BF16)
    xn_ref[...] = xn
    bd = lax.dot_general(xn, wbd_ref[...], (((1,), (1,)), ((), ())), preferred_element_type=F32)
    neg_a = -jnp.exp(par_ref[0:1, :])
    z = bd + par_ref[1:2, :]
    softplus = jnp.maximum(z, 0.0) + jnp.log1p(jnp.exp(-jnp.abs(z)))
    lane = lax.broadcasted_iota(jnp.int32, bd.shape, 1) % LANES
    gb_ref[...] = jnp.where(lane < 16, _sigmoid(bd), neg_a * softplus)


def _norm_gate(x2, w_norm, w_bd_t, par, tm=256):
    n = x2.shape[0]
    gw = w_bd_t.shape[0]
    return pl.pallas_call(
        _norm_gate_kernel,
        grid=(n // tm,),
        in_specs=[
            pl.BlockSpec((tm, D_MODEL), lambda i: (i, 0)),
            pl.BlockSpec((1, D_MODEL), lambda i: (0, 0)),
            pl.BlockSpec((gw, D_MODEL), lambda i: (0, 0)),
            pl.BlockSpec((8, gw), lambda i: (0, 0)),
        ],
        out_specs=[
            pl.BlockSpec((tm, D_MODEL), lambda i: (i, 0)),
            pl.BlockSpec((tm, gw), lambda i: (i, 0)),
        ],
        out_shape=[
            jax.ShapeDtypeStruct((n, D_MODEL), BF16),
            jax.ShapeDtypeStruct((n, gw), F32),
        ],
        compiler_params=_cparams(("parallel",)),
        name="norm_gate",
    )(x2, w_norm, w_bd_t, par)


def _matmul_nt_kernel(a_ref, bt_ref, o_ref):
    o_ref[...] = lax.dot_general(a_ref[...], bt_ref[...], (((1,), (1,)), ((), ())),
                                 preferred_element_type=F32).astype(o_ref.dtype)


def _matmul_nt(a, b_t, out_dtype, tm, tn, name):
    m, k = a.shape
    n = b_t.shape[0]
    return pl.pallas_call(
        _matmul_nt_kernel,
        grid=(m // tm, n // tn),
        in_specs=[
            pl.BlockSpec((tm, k), lambda i, j: (i, 0)),
            pl.BlockSpec((tn, k), lambda i, j: (j, 0)),
        ],
        out_specs=pl.BlockSpec((tm, tn), lambda i, j: (i, j)),
        out_shape=jax.ShapeDtypeStruct((m, n), out_dtype),
        compiler_params=_cparams(("parallel", "parallel")),
        name=name,
    )(a, b_t)


def _rope_tables(pos_ref, invf_ref):
    ang = pos_ref[...].astype(F32) * invf_ref[...]
    cos = jnp.cos(ang)
    sin = jnp.sin(ang)
    lane = lax.broadcasted_iota(jnp.int32, ang.shape, 1)
    c = jnp.where(lane < QK_ROPE, cos, 0.0)
    s_lo = jnp.where(lane < QK_ROPE // 2, -sin, 0.0)
    s_hi = jnp.where((lane >= QK_ROPE // 2) & (lane < QK_ROPE), sin, 0.0)
    return c, s_lo, s_hi


def _rope(r, tables):
    c, s_lo, s_hi = tables
    half = QK_ROPE // 2
    return r * c + pltpu.roll(r, LANES - half, 1) * s_lo + pltpu.roll(r, half, 1) * s_hi


def _rms_rows(c, gain):
    ms = jnp.mean(c * c, axis=-1, keepdims=True)
    return c * lax.rsqrt(ms + RMS_EPS) * gain


def _mla_prep_kernel(cq_ref, ckv_ref, kr_ref, pos_ref, invf_ref, glq_ref, glkv_ref, wq_ref, wkn_ref, wvt_ref,
                     gqh_ref, gkn_ref, gkr_ref, q_ref, k_ref, vt_ref):
    heads = range(MLA_HEADS)
    tables = _rope_tables(pos_ref, invf_ref)
    cq = _rms_rows(cq_ref[...].astype(F32), glq_ref[...]).astype(BF16)
    ckv = _rms_rows(ckv_ref[...].astype(F32), glkv_ref[...]).astype(BF16)
    q = jnp.dot(cq, wq_ref[...], preferred_element_type=F32)
    k_nope = jnp.dot(ckv, wkn_ref[...], preferred_element_type=F32)
    vt = lax.dot_general(wvt_ref[...], ckv, (((1,), (1,)), ((), ())), preferred_element_type=F32)
    ones_row = lax.broadcasted_iota(jnp.int32, vt.shape, 0) % VT_ROWS == V_HEAD
    vt_ref[...] = jnp.where(ones_row, 1.0, vt).astype(BF16)
    kr = kr_ref[...].astype(F32)
    ss_r = jnp.sum(kr * kr, axis=-1, keepdims=True)
    k_rot = _rope(kr * gkr_ref[...], tables)
    g_qn = gqh_ref[:, :QK_NOPE]
    g_qr = gqh_ref[:, QK_NOPE:]
    g_kn = gkn_ref[...]
    scale = QK_HEAD ** -0.5 * LOG2E

    qa = [q[:, h * QK_PAD:h * QK_PAD + QK_NOPE] for h in heads]
    qr = [q[:, h * QK_PAD + QK_NOPE:(h + 1) * QK_PAD] for h in heads]
    kn = [k_nope[:, h * QK_NOPE:(h + 1) * QK_NOPE] for h in heads]
    ss_q = [jnp.sum(a * a + r * r, axis=-1, keepdims=True) for a, r in zip(qa, qr)]
    ss_k = [jnp.sum(x * x, axis=-1, keepdims=True) + ss_r for x in kn]
    inv_q = [lax.rsqrt(s * (1.0 / QK_HEAD) + RMS_EPS) * scale for s in ss_q]
    inv_k = [lax.rsqrt(s * (1.0 / QK_HEAD) + RMS_EPS) for s in ss_k]
    q_rot = [_rope(r * g_qr, tables) for r in qr]
    for h in heads:
        q_ref[:, h * QK_PAD:h * QK_PAD + QK_NOPE] = (qa[h] * inv_q[h] * g_qn).astype(BF16)
        q_ref[:, h * QK_PAD + QK_NOPE:(h + 1) * QK_PAD] = (q_rot[h] * inv_q[h]).astype(BF16)
        k_ref[:, h * QK_PAD:h * QK_PAD + QK_NOPE] = (kn[h] * inv_k[h] * g_kn).astype(BF16)
        k_ref[:, h * QK_PAD + QK_NOPE:(h + 1) * QK_PAD] = (k_rot * inv_k[h]).astype(BF16)


def _mla_prep(proj, pos2, invf, g_q_lat, g_kv_lat, w_q, w_kn, w_vt, g_qh, g_kn, g_kr, batch, seq, tm=256):
    n = proj.shape[0]
    width = MLA_HEADS * QK_PAD
    ns = seq // tm
    tok = lambda col: (lambda b, i: (b * ns + i, col))
    const = lambda b, i: (0, 0)
    return pl.pallas_call(
        _mla_prep_kernel,
        grid=(batch, ns),
        in_specs=[
            pl.BlockSpec((tm, Q_LORA), tok(OFF_CQ // Q_LORA)),
            pl.BlockSpec((tm, KV_LORA), tok(OFF_CKV // KV_LORA)),
            pl.BlockSpec((tm, LANES), tok(OFF_KROPE // LANES)),
            pl.BlockSpec((tm, 1), tok(0)),
            pl.BlockSpec((1, LANES), const),
            pl.BlockSpec((1, Q_LORA), const),
            pl.BlockSpec((1, KV_LORA), const),
            pl.BlockSpec((Q_LORA, width), const),
            pl.BlockSpec((KV_LORA, MLA_HEADS * QK_NOPE), const),
            pl.BlockSpec((MLA_HEADS * VT_ROWS, KV_LORA), const),
            pl.BlockSpec((1, QK_PAD), const),
            pl.BlockSpec((1, LANES), const),
            pl.BlockSpec((1, LANES), const),
        ],
        out_specs=[
            pl.BlockSpec((tm, width), tok(0)),
            pl.BlockSpec((tm, width), tok(0)),
            pl.BlockSpec((MLA_HEADS * VT_ROWS, tm), lambda b, i: (b, i)),
        ],
        out_shape=[
            jax.ShapeDtypeStruct((n, width), BF16),
            jax.ShapeDtypeStruct((n, width), BF16),
            jax.ShapeDtypeStruct((batch * MLA_HEADS * VT_ROWS, seq), BF16),
        ],
        compiler_params=_cparams(("parallel", "parallel")),
        name="mla_prep",
    )(proj, proj, proj, pos2, invf, g_q_lat, g_kv_lat, w_q, w_kn, w_vt, g_qh, g_kn, g_kr)


def _attn_kernel(q_ref, k_ref, vt_ref, z_ref, o_ref, *, seq, tq):
    nq = seq // tq
    hq = tq // 2

    def chunk_mask(n_keys, query_offset):
        key_chunk = lax.broadcasted_iota(jnp.int32, (n_keys, hq), 0) // CHUNK
        qry_chunk = (lax.broadcasted_iota(jnp.int32, (n_keys, hq), 1) + query_offset) // CHUNK
        return key_chunk <= qry_chunk

    diag_mask_x = chunk_mask(hq, 0)
    diag_mask_y = chunk_mask(tq, hq)

    def qk(q, k):
        return lax.dot_general(k, q, (((1,), (1,)), ((), ())), preferred_element_type=F32)

    def softmax_step(st, carry, mask):
        m, acc = carry
        if mask is not None:
            st = jnp.where(mask, st, -jnp.inf)
        m_new = jnp.maximum(m, jnp.max(st, axis=0, keepdims=True))
        alpha = jnp.exp2(m - m_new)
        p = jnp.exp2(st - m_new)
        return p.astype(BF16), alpha, (m_new, acc)

    def pv(vt, p, alpha, carry):
        m, acc = carry
        return m, alpha * acc + jnp.dot(vt, p, preferred_element_type=F32)

    def full_tile(off):
        half = (k_ref[off:off + tq, :], vt_ref[:, off:off + tq], None)
        return half, half

    def diag_tile(off):
        return ((k_ref[off:off + hq, :], vt_ref[:, off:off + hq], diag_mask_x),
                (k_ref[off:off + tq, :], vt_ref[:, off:off + tq], diag_mask_y))

    steps = [(i, t) for i in range(nq) for t in range(i + 1)]
    qs, carries, tiles, scores, probs = {}, {}, {}, {}, {}

    def emit_qk(s):
        i, t = steps[s]
        if t == 0:
            qs[i] = (q_ref[i * tq:i * tq + hq, :], q_ref[i * tq + hq:(i + 1) * tq, :])
            init = (jnp.full((1, hq), -1e30, F32), jnp.zeros((VT_ROWS, hq), F32))
            carries[i] = [init, init]
        tiles[s] = diag_tile(i * tq) if t == i else full_tile(t * tq)
        scores[s] = [qk(qs[i][x], tiles[s][x][0]) for x in range(2)]

    def emit_softmax(s):
        i, _ = steps[s]
        st = scores.pop(s)
        out = []
        for x in range(2):
            p, alpha, carries[i][x] = softmax_step(st[x], carries[i][x], tiles[s][x][2])
            out.append((p, alpha))
        probs[s] = out

    def emit_pv(s):
        i, t = steps[s]
        tile = tiles.pop(s)
        for x, (p, alpha) in enumerate(probs.pop(s)):
            carries[i][x] = pv(tile[x][1], p, alpha, carries[i][x])
        if t == i:
            for x, (_, acc) in enumerate(carries.pop(i)):
                rows = slice(i * tq + x * hq, i * tq + (x + 1) * hq)
                o = (acc[:V_HEAD] / acc[V_HEAD:V_HEAD + 1]).T
                o_ref[rows, :] = (o * _silu(z_ref[rows, :].astype(F32))).astype(o_ref.dtype)

    emit_qk(0)
    for s in range(len(steps)):
        if s + 1 < len(steps):
            emit_qk(s + 1)
        emit_softmax(s)
        if s > 0:
            emit_pv(s - 1)
    emit_pv(len(steps) - 1)


def _attention(q, k, vt, proj, batch, seq, tq=512):
    n = q.shape[0]
    return pl.pallas_call(
        functools.partial(_attn_kernel, seq=seq, tq=tq),
        grid=(batch, MLA_HEADS),
        in_specs=[
            pl.BlockSpec((seq, QK_PAD), lambda b, h: (b, h)),
            pl.BlockSpec((seq, QK_PAD), lambda b, h: (b, h)),
            pl.BlockSpec((VT_ROWS, seq), lambda b, h: (b * MLA_HEADS + h, 0)),
            pl.BlockSpec((seq, V_HEAD), lambda b, h: (b, OFF_Z_MLA // V_HEAD + h)),
        ],
        out_specs=pl.BlockSpec((seq, V_HEAD), lambda b, h: (b, h)),
        out_shape=jax.ShapeDtypeStruct((n, MLA_WIDTH), BF16),
        compiler_params=_cparams(("parallel", "parallel")),
        name="mla_attention",
    )(q, k, vt, proj)


def _gdn_prep_kernel(q_ref, k_ref, v_ref, qh_ref, kh_ref, vh_ref, cwq_ref, cwk_ref, cwv_ref, gb_ref,
                     u_ref, w_ref, qd_ref, ai_ref, kdt_ref, gl_ref, *, tt):
    first = pl.program_id(1) == 0
    blk = GDN_BLOCK
    nb = tt // blk
    pair = 2 * LANES
    halo_rows = 8

    gb = gb_ref[...]
    rmod = lax.broadcasted_iota(jnp.int32, (tt, LANES), 0) % blk
    gc = gb
    s = 1
    while s < blk:
        gc = gc + jnp.where(rmod >= s, pltpu.roll(gc, s, 0), 0.0)
        s *= 2
    gc_t = [gc[p * blk:(p + 1) * blk].T for p in range(nb)]

    def conv_silu(x_ref, h_ref, cw_ref):
        halo = h_ref[...].astype(F32)[16 - halo_rows:16]
        xe = jnp.concatenate([jnp.where(first, 0.0, halo), x_ref[...].astype(F32)], axis=0)
        cw = cw_ref[...]
        y = xe * cw[CONV_K - 1:CONV_K]
        for sh in range(1, CONV_K):
            y = y + pltpu.roll(xe, sh, 0) * cw[CONV_K - 1 - sh:CONV_K - sh]
        y = y[halo_rows:]
        return _silu(y)

    def l2norm_heads(x, scale):
        parts = []
        for hh in range(GDN_GROUP):
            xs = x[:, hh * LANES:(hh + 1) * LANES]
            parts.append(xs * (lax.rsqrt(jnp.sum(xs * xs, axis=-1, keepdims=True) + L2_EPS) * scale))
        return jnp.concatenate(parts, axis=1)

    def lane_bcast(x, lane0):
        return jnp.concatenate(
            [jnp.broadcast_to(x[:, lane0 + hh:lane0 + hh + 1], (tt, LANES)) for hh in range(GDN_GROUP)], axis=1)

    q = l2norm_heads(conv_silu(q_ref, qh_ref, cwq_ref), GDN_DK ** -0.5)
    k = l2norm_heads(conv_silu(k_ref, kh_ref, cwk_ref), 1.0)
    v = conv_silu(v_ref, vh_ref, cwv_ref)
    beta_b = lane_bcast(gb, 0)
    gcol_b = lane_bcast(gc, 16)
    egc = jnp.exp(gcol_b)
    kb = k * beta_b
    vb = v * beta_b
    kbe = kb * egc
    qd_ref[...] = (q * egc).astype(BF16)
    glast_b = jnp.concatenate(
        [jnp.broadcast_to(gcol_b[(p + 1) * blk - 1:(p + 1) * blk, :], (blk, gcol_b.shape[1])) for p in range(nb)], axis=0)
    kd = k * jnp.exp(glast_b - gcol_b)
    for p in range(nb):
        r = slice(p * blk, (p + 1) * blk)
        gl_ref[p * 8:(p + 1) * 8, :] = jnp.exp(glast_b[p * blk:p * blk + 8, :])
        for hh in range(GDN_GROUP):
            c = slice(hh * LANES, (hh + 1) * LANES)
            kdt_ref[c, r] = kd[r, c].T.astype(BF16)

    ri = lax.broadcasted_iota(jnp.int32, (blk, pair), 0)
    ci = lax.broadcasted_iota(jnp.int32, (blk, pair), 1) % LANES
    causal = ri >= ci
    strict = ri > ci
    zero = jnp.zeros((blk, LANES), BF16)

    def bdiag(x):
        return jnp.concatenate([jnp.concatenate([x[:, :LANES], zero], axis=1),
                                jnp.concatenate([zero, x[:, LANES:]], axis=1)], axis=0)

    def mm(a, b):
        return jnp.dot(a, b, preferred_element_type=F32)

    items = [(p, j) for p in range(nb) for j in range(GDN_GROUP // 2)]
    rc = lambda p, j: (slice(p * blk, (p + 1) * blk), slice(j * pair, (j + 1) * pair))

    kq = []
    for p, j in items:
        r, c = rc(p, j)
        lhs = jnp.concatenate([kb[r, c].astype(BF16), q[r, c].astype(BF16)], axis=0)
        kq.append(lax.dot_general(lhs, bdiag(k[r, c].astype(BF16)), (((1,), (1,)), ((), ())),
                                  preferred_element_type=F32))
    e = []
    for (p, j), kq_i in zip(items, kq):
        r, c = rc(p, j)
        grow = jnp.concatenate([gc_t[p][16 + 2 * j:17 + 2 * j, :], gc_t[p][17 + 2 * j:18 + 2 * j, :]], axis=1)
        diff = gcol_b[r, c] - grow
        decay = jnp.where(causal, jnp.exp(jnp.where(causal, diff, 0.0)), 0.0)
        ai_ref[r, c] = (kq_i[blk:] * decay).astype(BF16)
        e.append(jnp.where(strict, -(kq_i[:blk] * decay), 0.0))
    eb = [x.astype(BF16) for x in e]
    pw = [mm(x, bdiag(x)) for x in eb]
    span = 2
    while span * 2 < blk:
        pwb = [x.astype(BF16) for x in pw]
        both = [mm(jnp.concatenate([e_i.astype(BF16), p_i], axis=0), bdiag(p_i)) for e_i, p_i in zip(e, pwb)]
        e = [e_i + p_i + b_i[:blk] for e_i, p_i, b_i in zip(e, pw, both)]
        pw = [b_i[blk:] for b_i in both]
        span *= 2
    last = [mm(e_i.astype(BF16), bdiag(p_i.astype(BF16))) for e_i, p_i in zip(e, pw)]
    e = [e_i + p_i + l_i for e_i, p_i, l_i in zip(e, pw, last)]
    for (p, j), e_i in zip(items, e):
        r, _ = rc(p, j)
        for s in range(2):
            c = slice((2 * j + s) * LANES, (2 * j + s + 1) * LANES)
            rhs = jnp.concatenate([vb[r, c].astype(BF16), kbe[r, c].astype(BF16)], axis=1)
            sol = mm(e_i[:, s * LANES:(s + 1) * LANES].astype(BF16), rhs)
            u_ref[r, c] = vb[r, c] + sol[:, :LANES]
            w_ref[r, c] = (kbe[r, c] + sol[:, LANES:]).astype(BF16)


def _gdn_prep(proj, conv_w, gb, batch, seq, tt=512):
    n = proj.shape[0]
    gw = GDN_GROUP * LANES
    ns = seq // tt
    ngrp = GDN_HEADS // GDN_GROUP
    col = lambda base: (lambda b, i, g: (b * ns + i, base // gw + g))
    halo = lambda base: (lambda b, i, g: (jnp.maximum((b * ns + i) * (tt // 16) - 1, 0), base // gw + g))
    cwc = lambda base: (lambda b, i, g: (0, base // gw + g))
    tok = lambda b, i, g: (b * ns + i, g)
    return pl.pallas_call(
        functools.partial(_gdn_prep_kernel, tt=tt),
        grid=(batch, ns, ngrp),
        in_specs=[
            pl.BlockSpec((tt, gw), col(OFF_QKV)),
            pl.BlockSpec((tt, gw), col(OFF_QKV + GDN_KEY_WIDTH)),
            pl.BlockSpec((tt, gw), col(OFF_QKV + 2 * GDN_KEY_WIDTH)),
            pl.BlockSpec((16, gw), halo(OFF_QKV)),
            pl.BlockSpec((16, gw), halo(OFF_QKV + GDN_KEY_WIDTH)),
            pl.BlockSpec((16, gw), halo(OFF_QKV + 2 * GDN_KEY_WIDTH)),
            pl.BlockSpec((CONV_K, gw), cwc(0)),
            pl.BlockSpec((CONV_K, gw), cwc(GDN_KEY_WIDTH)),
            pl.BlockSpec((CONV_K, gw), cwc(2 * GDN_KEY_WIDTH)),
            pl.BlockSpec((tt, LANES), tok),
        ],
        out_specs=[
            pl.BlockSpec((tt, gw), tok),
            pl.BlockSpec((tt, gw), tok),
            pl.BlockSpec((tt, gw), tok),
            pl.BlockSpec((tt, gw), tok),
            pl.BlockSpec((gw, tt), lambda b, i, g: (b * ngrp + g, i)),
            pl.BlockSpec((tt // GDN_BLOCK * 8, gw), tok),
        ],
        out_shape=[
            jax.ShapeDtypeStruct((n, GDN_WIDTH), F32),
            jax.ShapeDtypeStruct((n, GDN_WIDTH), BF16),
            jax.ShapeDtypeStruct((n, GDN_WIDTH), BF16),
            jax.ShapeDtypeStruct((n, GDN_WIDTH), BF16),
            jax.ShapeDtypeStruct((batch * GDN_HEADS * GDN_DK, seq), BF16),
            jax.ShapeDtypeStruct((n // GDN_BLOCK * 8, GDN_WIDTH), F32),
        ],
        compiler_params=_cparams(("parallel", "parallel", "parallel")),
        name="gdn_prep",
    )(proj, proj, proj, proj, proj, proj, conv_w, conv_w, conv_w, gb)


def _gdn_scan_kernel(u_ref, w_ref, qd_ref, ai_ref, kdt_ref, gl_ref, z_ref, g_ref, o_ref, s_ref, *, ts, hg):
    @pl.when(pl.program_id(2) == 0)
    def _():
        s_ref[...] = jnp.zeros_like(s_ref)

    blk = GDN_BLOCK
    gain = g_ref[...]
    for p in range(ts // blk):
        r = slice(p * blk, (p + 1) * blk)
        for hh in range(hg):
            c = slice(hh * LANES, (hh + 1) * LANES)
            state = s_ref[hh]
            lhs1 = jnp.concatenate([w_ref[r, c], qd_ref[r, c]], axis=0)
            r1 = jnp.dot(lhs1, state.astype(BF16), preferred_element_type=F32)
            v_new = u_ref[r, c] - r1[:blk]
            lhs2 = jnp.concatenate([ai_ref[r, c], kdt_ref[c, r]], axis=0)
            r2 = jnp.dot(lhs2, v_new.astype(BF16), preferred_element_type=F32)
            o = r1[blk:] + r2[:blk]
            gl = jnp.concatenate([gl_ref[p * 8:(p + 1) * 8, c]] * (blk // 8), axis=0)
            s_ref[hh] = state * gl + r2[blk:]
            z = z_ref[r, c].astype(F32)
            o_ref[r, c] = (_rms_rows(o, gain) * _silu(z)).astype(o_ref.dtype)


def _gdn_scan(u, w, qd, ai, kdt, gl8, proj, g_out, batch, seq, ts=512, hg=8):
    n = u.shape[0]
    gw = hg * LANES
    ns = seq // ts
    ngrp = GDN_HEADS // hg
    tok = lambda b, g, i: (b * ns + i, g)
    return pl.pallas_call(
        functools.partial(_gdn_scan_kernel, ts=ts, hg=hg),
        grid=(batch, ngrp, ns),
        in_specs=[
            pl.BlockSpec((ts, gw), tok),
            pl.BlockSpec((ts, gw), tok),
            pl.BlockSpec((ts, gw), tok),
            pl.BlockSpec((ts, gw), tok),
            pl.BlockSpec((gw, ts), lambda b, g, i: (b * ngrp + g, i)),
            pl.BlockSpec((ts // GDN_BLOCK * 8, gw), tok),
            pl.BlockSpec((ts, gw), lambda b, g, i: (b * ns + i, OFF_Z_GDN // gw + g)),
            pl.BlockSpec((1, LANES), lambda b, g, i: (0, 0)),
        ],
        out_specs=pl.BlockSpec((ts, gw), tok),
        out_shape=jax.ShapeDtypeStruct((n, GDN_WIDTH), BF16),
        scratch_shapes=[pltpu.VMEM((hg, GDN_DK, GDN_DV), F32)],
        compiler_params=_cparams(("parallel", "parallel", "arbitrary")),
        name="gdn_scan",
    )(u, w, qd, ai, kdt, gl8, proj, g_out)


def _merge_kernel(a1_ref, a2_ref, w1_ref, w2_ref, g1_ref, g2_ref, b_ref, h_ref):
    y1 = jnp.dot(a1_ref[...], w1_ref[...], preferred_element_type=F32)
    y2 = jnp.dot(a2_ref[...], w2_ref[...], preferred_element_type=F32)
    s1 = _sigmoid(g1_ref[...].astype(F32) + b_ref[0:1, :])
    s2 = _sigmoid(g2_ref[...].astype(F32) + b_ref[1:2, :])
    h_ref[...] = (s1 * y1 + s2 * y2).astype(h_ref.dtype)


def _merge(o_mla, o_gdn, w_o_mla, w_o_gdn, proj, b_gate, tm=1024, tn=1024):
    n = o_mla.shape[0]
    return pl.pallas_call(
        _merge_kernel,
        grid=(n // tm, D_MODEL // tn),
        in_specs=[
            pl.BlockSpec((tm, MLA_WIDTH), lambda i, j: (i, 0)),
            pl.BlockSpec((tm, GDN_WIDTH), lambda i, j: (i, 0)),
            pl.BlockSpec((MLA_WIDTH, tn), lambda i, j: (0, j)),
            pl.BlockSpec((GDN_WIDTH, tn), lambda i, j: (0, j)),
            pl.BlockSpec((tm, tn), lambda i, j: (i, OFF_GATE_MLA // tn + j)),
            pl.BlockSpec((tm, tn), lambda i, j: (i, OFF_GATE_GDN // tn + j)),
            pl.BlockSpec((2, tn), lambda i, j: (0, j)),
        ],
        out_specs=pl.BlockSpec((tm, tn), lambda i, j: (i, j)),
        out_shape=jax.ShapeDtypeStruct((n, D_MODEL), BF16),
        compiler_params=_cparams(("parallel", "parallel")),
        name="merge",
    )(o_mla, o_gdn, w_o_mla, w_o_gdn, proj, proj, b_gate)


def _out_kernel(h_ref, w_ref, x_ref, o_ref):
    o_ref[...] = x_ref[...] + jnp.dot(h_ref[...], w_ref[...], preferred_element_type=F32)


def _out_proj(h, w_out, x2, tm=1024, tn=1024):
    n = h.shape[0]
    return pl.pallas_call(
        _out_kernel,
        grid=(n // tm, D_MODEL // tn),
        in_specs=[
            pl.BlockSpec((tm, D_MODEL), lambda i, j: (i, 0)),
            pl.BlockSpec((D_MODEL, tn), lambda i, j: (0, j)),
            pl.BlockSpec((tm, tn), lambda i, j: (i, j)),
        ],
        out_specs=pl.BlockSpec((tm, tn), lambda i, j: (i, j)),
        out_shape=jax.ShapeDtypeStruct((n, D_MODEL), F32),
        compiler_params=_cparams(("parallel", "parallel")),
        name="out_proj",
    )(h, w_out, x2)


def _pad_cols(a, width):
    return jnp.pad(a, ((0, 0), (0, width - a.shape[1])))


def kernel(x, positions, w_norm, w_in, b_gate, w_q_up, w_kv_up, g_q_latent, g_kv_latent, g_q_head, g_k_head,
           w_o_mla, conv_w, a_log, dt_bias, g_gdn_out, w_o_gdn, w_out):
    batch, seq, _ = x.shape
    n = batch * seq
    x2 = x.reshape(n, D_MODEL)

    o_cq, o_ckv, o_kr = 0, Q_LORA, Q_LORA + KV_LORA
    o_zm = o_kr + QK_ROPE
    o_qkv = o_zm + MLA_WIDTH
    o_b = o_qkv + GDN_QKV
    o_a = o_b + GDN_HEADS
    o_zg = o_a + GDN_HEADS
    o_gm = o_zg + GDN_WIDTH
    o_gg = o_gm + D_MODEL
    w_t = w_in.T
    rows = lambda lo, count: w_t[lo:lo + count]
    w_in_pt = jnp.concatenate([
        rows(o_gm, D_MODEL), rows(o_gg, D_MODEL), rows(o_qkv, GDN_QKV), rows(o_zm, MLA_WIDTH), rows(o_zg, GDN_WIDTH),
        rows(o_cq, Q_LORA), rows(o_ckv, KV_LORA), rows(o_kr, QK_ROPE),
        jnp.zeros((PROJ_WIDTH - OFF_KROPE - QK_ROPE, D_MODEL), w_in.dtype)], axis=0).astype(BF16)

    ngrp = GDN_HEADS // GDN_GROUP

    def gate_rows(beta_part, decay_part):
        width = beta_part.shape[1]
        out = []
        for g in range(ngrp):
            hs = slice(g * GDN_GROUP, (g + 1) * GDN_GROUP)
            out += [beta_part[hs], jnp.zeros((16 - GDN_GROUP, width), F32),
                    decay_part[hs], jnp.zeros((LANES - 16 - GDN_GROUP, width), F32)]
        return jnp.concatenate(out, axis=0)

    w_bd_t = gate_rows(rows(o_b, GDN_HEADS), rows(o_a, GDN_HEADS)).astype(BF16)
    zeros_h = jnp.zeros((GDN_HEADS, 1), F32)
    par = jnp.concatenate([gate_rows(zeros_h, a_log[:, None]).T, gate_rows(zeros_h, dt_bias[:, None]).T,
                           jnp.zeros((6, ngrp * LANES), F32)], axis=0)

    w_q = jnp.pad(w_q_up.reshape(Q_LORA, MLA_HEADS, QK_HEAD), ((0, 0), (0, 0), (0, QK_PAD - QK_HEAD)))
    w_q = w_q.reshape(Q_LORA, MLA_HEADS * QK_PAD).astype(BF16)
    w_kv3 = w_kv_up.reshape(KV_LORA, MLA_HEADS, QK_NOPE + V_HEAD)
    w_kn = w_kv3[:, :, :QK_NOPE].reshape(KV_LORA, MLA_HEADS * QK_NOPE).astype(BF16)
    w_vt = jnp.pad(w_kv3[:, :, QK_NOPE:], ((0, 0), (0, 0), (0, VT_ROWS - V_HEAD)))
    w_vt = w_vt.reshape(KV_LORA, MLA_HEADS * VT_ROWS).T.astype(BF16)
    g_qh = _pad_cols(g_q_head[None, :], QK_PAD)
    g_kn = g_k_head[None, :QK_NOPE]
    g_kr = _pad_cols(g_k_head[None, QK_NOPE:], LANES)
    inv_freq = ROPE_THETA ** (-jnp.arange(0, QK_ROPE, 2, dtype=F32) / QK_ROPE)
    invf = jnp.tile(inv_freq, LANES // (QK_ROPE // 2))[None, :]
    pos2 = positions.reshape(n, 1)

    xn, gb = _norm_gate(x2, w_norm[None, :], w_bd_t, par)
    proj = _matmul_nt(xn, w_in_pt, BF16, 1024, 1024, "in_proj")

    q, k, vt = _mla_prep(proj, pos2, invf, g_q_latent[None, :], g_kv_latent[None, :], w_q, w_kn, w_vt, g_qh, g_kn,
                         g_kr, batch, seq)
    o_mla = _attention(q, k, vt, proj, batch, seq)

    u, w, qd, ai, kdt, gl8 = _gdn_prep(proj, conv_w, gb, batch, seq)
    o_gdn = _gdn_scan(u, w, qd, ai, kdt, gl8, proj, g_gdn_out[None, :], batch, seq)

    h = _merge(o_mla, o_gdn, w_o_mla.astype(BF16), w_o_gdn.astype(BF16), proj, b_gate)
    out = _out_proj(h, w_out.astype(BF16), x2)
    return out.reshape(batch, seq, D_MODEL)
```

```python
import functools

import jax
import jax.numpy as jnp
from jax import lax
from jax.experimental import pallas as pl
from jax.experimental.pallas import tpu as pltpu

F32 = jnp.float32
BF16 = jnp.bfloat16

D_MODEL = 4096
CHUNK = 64
RMS_EPS = 1e-6
L2_EPS = 1e-6

MLA_HEADS = 16
QK_NOPE = 128
QK_ROPE = 64
QK_HEAD = QK_NOPE + QK_ROPE
V_HEAD = 128
Q_LORA = 1024
KV_LORA = 512
ROPE_THETA = 10000.0
MLA_WIDTH = MLA_HEADS * V_HEAD

GDN_HEADS = 16
GDN_DK = 128
GDN_DV = 128
GDN_KEY_WIDTH = GDN_HEADS * GDN_DK
GDN_WIDTH = GDN_HEADS * GDN_DV
GDN_QKV = 2 * GDN_KEY_WIDTH + GDN_WIDTH
CONV_K = 4

LANES = 128
QK_PAD = 256
GDN_BLOCK = 128
GDN_GROUP = 4

OFF_GATE_MLA = 0
OFF_GATE_GDN = OFF_GATE_MLA + D_MODEL
OFF_QKV = OFF_GATE_GDN + D_MODEL
OFF_Z_MLA = OFF_QKV + GDN_QKV
OFF_Z_GDN = OFF_Z_MLA + MLA_WIDTH
OFF_CQ = OFF_Z_GDN + GDN_WIDTH
OFF_CKV = OFF_CQ + Q_LORA
OFF_KROPE = OFF_CKV + KV_LORA
PROJ_WIDTH = 20480

VMEM_LIMIT = 56 * 1024 * 1024
LOG2E = 1.4426950408889634
VT_ROWS = 144


def _cparams(sem):
    return pltpu.CompilerParams(dimension_semantics=sem, vmem_limit_bytes=VMEM_LIMIT)


def _sigmoid(x):
    return 1.0 / (1.0 + jnp.exp(-x))


def _silu(x):
    h = 0.5 * x
    return h + h * jnp.tanh(h)


def _norm_gate_kernel(x_ref, wn_ref, wbd_ref, par_ref, xn_ref, gb_ref):
    x = x_ref[...]
    ms = jnp.mean(x * x, axis=-1, keepdims=True)
    xn = (x * lax.rsqrt(ms + RMS_EPS) * wn_ref[...]).astype(BF16)
    xn_ref[...] = xn
    bd = lax.dot_general(xn, wbd_ref[...], (((1,), (1,)), ((), ())), preferred_element_type=F32)
    neg_a = -jnp.exp(par_ref[0:1, :])
    z = bd + par_ref[1:2, :]
    softplus = jnp.maximum(z, 0.0) + jnp.log1p(jnp.exp(-jnp.abs(z)))
    lane = lax.broadcasted_iota(jnp.int32, bd.shape, 1) % LANES
    gb_ref[...] = jnp.where(lane < 16, _sigmoid(bd), neg_a * softplus)


def _norm_gate(x2, w_norm, w_bd_t, par, tm=512):
    n = x2.shape[0]
    gw = w_bd_t.shape[0]
    return pl.pallas_call(
        _norm_gate_kernel,
        grid=(n // tm,),
        in_specs=[
            pl.BlockSpec((tm, D_MODEL), lambda i: (i, 0)),
            pl.BlockSpec((1, D_MODEL), lambda i: (0, 0)),
            pl.BlockSpec((gw, D_MODEL), lambda i: (0, 0)),
            pl.BlockSpec((8, gw), lambda i: (0, 0)),
        ],
        out_specs=[
            pl.BlockSpec((tm, D_MODEL), lambda i: (i, 0)),
            pl.BlockSpec((tm, gw), lambda i: (i, 0)),
        ],
        out_shape=[
            jax.ShapeDtypeStruct((n, D_MODEL), BF16),
            jax.ShapeDtypeStruct((n, gw), F32),
        ],
        compiler_params=_cparams(("parallel",)),
        name="norm_gate",
    )(x2, w_norm, w_bd_t, par)


def _matmul_nt_kernel(a_ref, bt_ref, o_ref):
    o_ref[...] = lax.dot_general(a_ref[...], bt_ref[...], (((1,), (1,)), ((), ())),
                                 preferred_element_type=F32).astype(o_ref.dtype)


def _matmul_nt(a, b_t, out_dtype, tm, tn, name):
    m, k = a.shape
    n = b_t.shape[0]
    return pl.pallas_call(
        _matmul_nt_kernel,
        grid=(m // tm, n // tn),
        in_specs=[
            pl.BlockSpec((tm, k), lambda i, j: (i, 0)),
            pl.BlockSpec((tn, k), lambda i, j: (j, 0)),
        ],
        out_specs=pl.BlockSpec((tm, tn), lambda i, j: (i, j)),
        out_shape=jax.ShapeDtypeStruct((m, n), out_dtype),
        compiler_params=_cparams(("parallel", "parallel")),
        name=name,
    )(a, b_t)


def _rope_tables(pos_ref, invf_ref):
    ang = pos_ref[...].astype(F32) * invf_ref[...]
    cos = jnp.cos(ang)
    sin = jnp.sin(ang)
    lane = lax.broadcasted_iota(jnp.int32, ang.shape, 1)
    c = jnp.where(lane < QK_ROPE, cos, 0.0)
    s_lo = jnp.where(lane < QK_ROPE // 2, -sin, 0.0)
    s_hi = jnp.where((lane >= QK_ROPE // 2) & (lane < QK_ROPE), sin, 0.0)
    return c, s_lo, s_hi


def _rope(r, tables):
    c, s_lo, s_hi = tables
    half = QK_ROPE // 2
    return r * c + pltpu.roll(r, LANES - half, 1) * s_lo + pltpu.roll(r, half, 1) * s_hi


def _rope_dup(r, tables):
    c, s_lo, s_hi = tables
    return r * c + pltpu.roll(r, QK_ROPE // 2, 1) * (s_lo + s_hi)


def _rms_rows(c, gain):
    ms = jnp.mean(c * c, axis=-1, keepdims=True)
    return c * lax.rsqrt(ms + RMS_EPS) * gain


def _mla_prep_kernel(cq_ref, ckv_ref, kr_ref, pos_ref, invf_ref, glq_ref, glkv_ref, wq_ref, wkn_ref, wvt_ref,
                     gqh_ref, gkn_ref, gkr_ref, q_ref, k_ref, vt_ref):
    heads = range(MLA_HEADS)
    tables = _rope_tables(pos_ref, invf_ref)
    cq = _rms_rows(cq_ref[...].astype(F32), glq_ref[...]).astype(BF16)
    ckv = _rms_rows(ckv_ref[...].astype(F32), glkv_ref[...]).astype(BF16)
    q = jnp.dot(cq, wq_ref[...], preferred_element_type=F32)
    k_nope = jnp.dot(ckv, wkn_ref[...], preferred_element_type=F32)
    vt = lax.dot_general(wvt_ref[...], ckv, (((1,), (1,)), ((), ())), preferred_element_type=F32)
    ones_row = lax.broadcasted_iota(jnp.int32, vt.shape, 0) % VT_ROWS == V_HEAD
    vt_ref[...] = jnp.where(ones_row, 1.0, vt).astype(BF16)
    kr = kr_ref[...].astype(F32)
    ss_r = jnp.sum(kr * kr, axis=-1, keepdims=True)
    k_rot = _rope(kr * gkr_ref[...], tables)
    g_qn = gqh_ref[:, :QK_NOPE]
    g_qr = gqh_ref[:, QK_NOPE:]
    g_kn = gkn_ref[...]
    scale = QK_HEAD ** -0.5 * LOG2E

    qa = [q[:, h * QK_PAD:h * QK_PAD + QK_NOPE] for h in heads]
    qr = [q[:, h * QK_PAD + QK_NOPE:(h + 1) * QK_PAD] for h in heads]
    kn = [k_nope[:, h * QK_NOPE:(h + 1) * QK_NOPE] for h in heads]
    first_copy = (lax.broadcasted_iota(jnp.int32, (1, LANES), 1) < QK_ROPE).astype(F32)
    qm = [r * first_copy for r in qr]
    ss_q = [jnp.sum(a * a + r * r, axis=-1, keepdims=True) for a, r in zip(qa, qm)]
    ss_k = [jnp.sum(x * x, axis=-1, keepdims=True) + ss_r for x in kn]
    inv_q = [lax.rsqrt(s * (1.0 / QK_HEAD) + RMS_EPS) * scale for s in ss_q]
    inv_k = [lax.rsqrt(s * (1.0 / QK_HEAD) + RMS_EPS) for s in ss_k]
    q_rot = [_rope_dup(r * g_qr, tables) for r in qr]
    for h in heads:
        q_ref[:, h * QK_PAD:h * QK_PAD + QK_NOPE] = (qa[h] * inv_q[h] * g_qn).astype(BF16)
        q_ref[:, h * QK_PAD + QK_NOPE:(h + 1) * QK_PAD] = (q_rot[h] * inv_q[h]).astype(BF16)
        k_ref[:, h * QK_PAD:h * QK_PAD + QK_NOPE] = (kn[h] * inv_k[h] * g_kn).astype(BF16)
        k_ref[:, h * QK_PAD + QK_NOPE:(h + 1) * QK_PAD] = (k_rot * inv_k[h]).astype(BF16)


def _mla_prep(proj, pos2, invf, g_q_lat, g_kv_lat, w_q, w_kn, w_vt, g_qh, g_kn, g_kr, batch, seq, tm=256):
    n = proj.shape[0]
    width = MLA_HEADS * QK_PAD
    ns = seq // tm
    tok = lambda col: (lambda b, i: (b * ns + i, col))
    const = lambda b, i: (0, 0)
    return pl.pallas_call(
        _mla_prep_kernel,
        grid=(batch, ns),
        in_specs=[
            pl.BlockSpec((tm, Q_LORA), tok(OFF_CQ // Q_LORA)),
            pl.BlockSpec((tm, KV_LORA), tok(OFF_CKV // KV_LORA)),
            pl.BlockSpec((tm, LANES), tok(OFF_KROPE // LANES)),
            pl.BlockSpec((tm, 1), tok(0)),
            pl.BlockSpec((1, LANES), const),
            pl.BlockSpec((1, Q_LORA), const),
            pl.BlockSpec((1, KV_LORA), const),
            pl.BlockSpec((Q_LORA, width), const),
            pl.BlockSpec((KV_LORA, MLA_HEADS * QK_NOPE), const),
            pl.BlockSpec((MLA_HEADS * VT_ROWS, KV_LORA), const),
            pl.BlockSpec((1, QK_PAD), const),
            pl.BlockSpec((1, LANES), const),
            pl.BlockSpec((1, LANES), const),
        ],
        out_specs=[
            pl.BlockSpec((tm, width), tok(0)),
            pl.BlockSpec((tm, width), tok(0)),
            pl.BlockSpec((MLA_HEADS * VT_ROWS, tm), lambda b, i: (b, i)),
        ],
        out_shape=[
            jax.ShapeDtypeStruct((n, width), BF16),
            jax.ShapeDtypeStruct((n, width), BF16),
            jax.ShapeDtypeStruct((batch * MLA_HEADS * VT_ROWS, seq), BF16),
        ],
        compiler_params=_cparams(("parallel", "parallel")),
        name="mla_prep",
    )(proj, proj, proj, pos2, invf, g_q_lat, g_kv_lat, w_q, w_kn, w_vt, g_qh, g_kn, g_kr)


def _attn_kernel(q_ref, k_ref, vt_ref, z_ref, o_ref, *, seq, tq):
    nq = seq // tq
    hq = tq // 2

    def chunk_mask(n_keys, query_offset):
        key_chunk = lax.broadcasted_iota(jnp.int32, (n_keys, hq), 0) // CHUNK
        qry_chunk = (lax.broadcasted_iota(jnp.int32, (n_keys, hq), 1) + query_offset) // CHUNK
        return key_chunk <= qry_chunk

    diag_mask_x = chunk_mask(hq, 0)
    diag_mask_y = chunk_mask(tq, hq)

    def qk(q, k):
        return lax.dot_general(k, q, (((1,), (1,)), ((), ())), preferred_element_type=F32)

    def softmax_step(st, carry, mask):
        m, acc = carry
        if mask is not None:
            st = jnp.where(mask, st, -jnp.inf)
        m_new = jnp.maximum(m, jnp.max(st, axis=0, keepdims=True))
        alpha = jnp.exp2(m - m_new)
        p = jnp.exp2(st - m_new)
        return p.astype(BF16), alpha, (m_new, acc)

    def pv(vt, p, alpha, carry):
        m, acc = carry
        return m, alpha * acc + jnp.dot(vt, p, preferred_element_type=F32)

    def full_tile(off):
        half = (k_ref[off:off + tq, :], vt_ref[:, off:off + tq], None)
        return half, half

    def diag_tile(off):
        return ((k_ref[off:off + hq, :], vt_ref[:, off:off + hq], diag_mask_x),
                (k_ref[off:off + tq, :], vt_ref[:, off:off + tq], diag_mask_y))

    steps = [(i, t) for i in range(nq) for t in range(i + 1)]
    qs, carries, tiles, scores, probs = {}, {}, {}, {}, {}

    def emit_qk(s):
        i, t = steps[s]
        if t == 0:
            qs[i] = (q_ref[i * tq:i * tq + hq, :], q_ref[i * tq + hq:(i + 1) * tq, :])
            init = (jnp.full((1, hq), -1e30, F32), jnp.zeros((VT_ROWS, hq), F32))
            carries[i] = [init, init]
        tiles[s] = diag_tile(i * tq) if t == i else full_tile(t * tq)
        scores[s] = [qk(qs[i][x], tiles[s][x][0]) for x in range(2)]

    def emit_softmax(s):
        i, _ = steps[s]
        st = scores.pop(s)
        out = []
        for x in range(2):
            p, alpha, carries[i][x] = softmax_step(st[x], carries[i][x], tiles[s][x][2])
            out.append((p, alpha))
        probs[s] = out

    def emit_pv(s):
        i, t = steps[s]
        tile = tiles.pop(s)
        for x, (p, alpha) in enumerate(probs.pop(s)):
            carries[i][x] = pv(tile[x][1], p, alpha, carries[i][x])
        if t == i:
            for x, (_, acc) in enumerate(carries.pop(i)):
                rows = slice(i * tq + x * hq, i * tq + (x + 1) * hq)
                o = (acc[:V_HEAD] / acc[V_HEAD:V_HEAD + 1]).T
                o_ref[rows, :] = (o * _silu(z_ref[rows, :].astype(F32))).astype(o_ref.dtype)

    emit_qk(0)
    for s in range(len(steps)):
        if s + 1 < len(steps):
            emit_qk(s + 1)
        emit_softmax(s)
        if s > 0:
            emit_pv(s - 1)
    emit_pv(len(steps) - 1)


def _attention(q, k, vt, proj, batch, seq, tq=512):
    n = q.shape[0]
    return pl.pallas_call(
        functools.partial(_attn_kernel, seq=seq, tq=tq),
        grid=(batch, MLA_HEADS),
        in_specs=[
            pl.BlockSpec((seq, QK_PAD), lambda b, h: (b, h)),
            pl.BlockSpec((seq, QK_PAD), lambda b, h: (b, h)),
            pl.BlockSpec((VT_ROWS, seq), lambda b, h: (b * MLA_HEADS + h, 0)),
            pl.BlockSpec((seq, V_HEAD), lambda b, h: (b, OFF_Z_MLA // V_HEAD + h)),
        ],
        out_specs=pl.BlockSpec((seq, V_HEAD), lambda b, h: (b, h)),
        out_shape=jax.ShapeDtypeStruct((n, MLA_WIDTH), BF16),
        compiler_params=_cparams(("parallel", "parallel")),
        name="mla_attention",
    )(q, k, vt, proj)


def _gdn_prep_kernel(q_ref, k_ref, v_ref, qh_ref, kh_ref, vh_ref, cwq_ref, cwk_ref, cwv_ref, gb_ref,
                     u_ref, w_ref, qd_ref, ai_ref, kdt_ref, gl_ref, *, tt):
    first = pl.program_id(1) == 0
    blk = GDN_BLOCK
    nb = tt // blk
    pair = 2 * LANES

    gb = gb_ref[...]
    rmod = lax.broadcasted_iota(jnp.int32, (tt, LANES), 0) % blk
    gc = gb
    s = 1
    while s < blk:
        gc = gc + jnp.where(rmod >= s, pltpu.roll(gc, s, 0), 0.0)
        s *= 2
    gc_t = [gc[p * blk:(p + 1) * blk].T for p in range(nb)]

    halo_rows = 16
    win = blk + halo_rows
    srow = lax.broadcasted_iota(jnp.int32, ((CONV_K - 1) * blk, win), 0)
    scol = lax.broadcasted_iota(jnp.int32, ((CONV_K - 1) * blk, win), 1)
    shift_mat = (scol == srow % blk + halo_rows - 1 - srow // blk).astype(BF16)

    def shifted(x_ref, h_ref):
        halo = jnp.where(first, jnp.zeros_like(h_ref[...]), h_ref[...])
        xe = jnp.concatenate([halo, x_ref[...]], axis=0)
        return [jnp.dot(shift_mat, xe[p * blk:p * blk + win], preferred_element_type=F32) for p in range(nb)]

    def conv_silu(x_ref, cw_ref, sh):
        cw = cw_ref[...]
        out = []
        for p in range(nb):
            y = x_ref[p * blk:(p + 1) * blk, :].astype(F32) * cw[CONV_K - 1:CONV_K]
            for s in range(CONV_K - 1):
                y = y + sh[p][s * blk:(s + 1) * blk] * cw[CONV_K - 2 - s:CONV_K - 1 - s]
            out.append(_silu(y))
        return jnp.concatenate(out, axis=0)

    def l2norm_heads(x, scale):
        parts = []
        for hh in range(GDN_GROUP):
            xs = x[:, hh * LANES:(hh + 1) * LANES]
            parts.append(xs * (lax.rsqrt(jnp.sum(xs * xs, axis=-1, keepdims=True) + L2_EPS) * scale))
        return jnp.concatenate(parts, axis=1)

    def lane_bcast(x, lane0):
        return jnp.concatenate(
            [jnp.broadcast_to(x[:, lane0 + hh:lane0 + hh + 1], (tt, LANES)) for hh in range(GDN_GROUP)], axis=1)

    sh_q = shifted(q_ref, qh_ref)
    sh_k = shifted(k_ref, kh_ref)
    q = l2norm_heads(conv_silu(q_ref, cwq_ref, sh_q), GDN_DK ** -0.5)
    sh_v = shifted(v_ref, vh_ref)
    k = l2norm_heads(conv_silu(k_ref, cwk_ref, sh_k), 1.0)
    v = conv_silu(v_ref, cwv_ref, sh_v)
    beta_b = lane_bcast(gb, 0)
    gcol_b = lane_bcast(gc, 16)
    egc = jnp.exp(gcol_b)
    kb = k * beta_b
    vb = v * beta_b
    kbe = kb * egc
    qd_ref[...] = (q * egc).astype(BF16)
    glast_b = jnp.concatenate(
        [jnp.broadcast_to(gcol_b[(p + 1) * blk - 1:(p + 1) * blk, :], (blk, gcol_b.shape[1])) for p in range(nb)], axis=0)
    kd = k * jnp.exp(glast_b - gcol_b)
    for p in range(nb):
        r = slice(p * blk, (p + 1) * blk)
        gl_ref[p * 8:(p + 1) * 8, :] = jnp.exp(glast_b[p * blk:p * blk + 8, :])
        for hh in range(GDN_GROUP):
            c = slice(hh * LANES, (hh + 1) * LANES)
            kdt_ref[c, r] = kd[r, c].T.astype(BF16)

    ri = lax.broadcasted_iota(jnp.int32, (blk, pair), 0)
    ci = lax.broadcasted_iota(jnp.int32, (blk, pair), 1) % LANES
    causal = ri >= ci
    strict = ri > ci
    zero = jnp.zeros((blk, LANES), BF16)

    def bdiag(x):
        return jnp.concatenate([jnp.concatenate([x[:, :LANES], zero], axis=1),
                                jnp.concatenate([zero, x[:, LANES:]], axis=1)], axis=0)

    def mm(a, b):
        return jnp.dot(a, b, preferred_element_type=F32)

    items = [(p, j) for p in range(nb) for j in range(GDN_GROUP // 2)]
    rc = lambda p, j: (slice(p * blk, (p + 1) * blk), slice(j * pair, (j + 1) * pair))

    kq = []
    for p, j in items:
        r, c = rc(p, j)
        lhs = jnp.concatenate([kb[r, c].astype(BF16), q[r, c].astype(BF16)], axis=0)
        kq.append(lax.dot_general(lhs, bdiag(k[r, c].astype(BF16)), (((1,), (1,)), ((), ())),
                                  preferred_element_type=F32))
    e = []
    for (p, j), kq_i in zip(items, kq):
        r, c = rc(p, j)
        grow = jnp.concatenate([gc_t[p][16 + 2 * j:17 + 2 * j, :], gc_t[p][17 + 2 * j:18 + 2 * j, :]], axis=1)
        diff = gcol_b[r, c] - grow
        decay = jnp.where(causal, jnp.exp(jnp.where(causal, diff, 0.0)), 0.0)
        ai_ref[r, c] = (kq_i[blk:] * decay).astype(BF16)
        e.append(jnp.where(strict, -(kq_i[:blk] * decay), 0.0))
    eb = [x.astype(BF16) for x in e]
    pw = [mm(x, bdiag(x)) for x in eb]
    span = 2
    while span * 2 < blk:
        pwb = [x.astype(BF16) for x in pw]
        both = [mm(jnp.concatenate([e_i.astype(BF16), p_i], axis=0), bdiag(p_i)) for e_i, p_i in zip(e, pwb)]
        e = [e_i + p_i + b_i[:blk] for e_i, p_i, b_i in zip(e, pw, both)]
        pw = [b_i[blk:] for b_i in both]
        span *= 2
    last = [mm(e_i.astype(BF16), bdiag(p_i.astype(BF16))) for e_i, p_i in zip(e, pw)]
    e = [e_i + p_i + l_i for e_i, p_i, l_i in zip(e, pw, last)]
    for (p, j), e_i in zip(items, e):
        r, _ = rc(p, j)
        for s in range(2):
            c = slice((2 * j + s) * LANES, (2 * j + s + 1) * LANES)
            rhs = jnp.concatenate([vb[r, c].astype(BF16), kbe[r, c].astype(BF16)], axis=1)
            sol = mm(e_i[:, s * LANES:(s + 1) * LANES].astype(BF16), rhs)
            u_ref[r, c] = vb[r, c] + sol[:, :LANES]
            w_ref[r, c] = (kbe[r, c] + sol[:, LANES:]).astype(BF16)


def _gdn_prep(proj, conv_w, gb, batch, seq, tt=1024):
    n = proj.shape[0]
    gw = GDN_GROUP * LANES
    ns = seq // tt
    ngrp = GDN_HEADS // GDN_GROUP
    col = lambda base: (lambda b, i, g: (b * ns + i, base // gw + g))
    halo = lambda base: (lambda b, i, g: (jnp.maximum((b * ns + i) * (tt // 16) - 1, 0), base // gw + g))
    cwc = lambda base: (lambda b, i, g: (0, base // gw + g))
    tok = lambda b, i, g: (b * ns + i, g)
    return pl.pallas_call(
        functools.partial(_gdn_prep_kernel, tt=tt),
        grid=(batch, ns, ngrp),
        in_specs=[
            pl.BlockSpec((tt, gw), col(OFF_QKV)),
            pl.BlockSpec((tt, gw), col(OFF_QKV + GDN_KEY_WIDTH)),
            pl.BlockSpec((tt, gw), col(OFF_QKV + 2 * GDN_KEY_WIDTH)),
            pl.BlockSpec((16, gw), halo(OFF_QKV)),
            pl.BlockSpec((16, gw), halo(OFF_QKV + GDN_KEY_WIDTH)),
            pl.BlockSpec((16, gw), halo(OFF_QKV + 2 * GDN_KEY_WIDTH)),
            pl.BlockSpec((CONV_K, gw), cwc(0)),
            pl.BlockSpec((CONV_K, gw), cwc(GDN_KEY_WIDTH)),
            pl.BlockSpec((CONV_K, gw), cwc(2 * GDN_KEY_WIDTH)),
            pl.BlockSpec((tt, LANES), tok),
        ],
        out_specs=[
            pl.BlockSpec((tt, gw), tok),
            pl.BlockSpec((tt, gw), tok),
            pl.BlockSpec((tt, gw), tok),
            pl.BlockSpec((tt, gw), tok),
            pl.BlockSpec((gw, tt), lambda b, i, g: (b * ngrp + g, i)),
            pl.BlockSpec((tt // GDN_BLOCK * 8, gw), tok),
        ],
        out_shape=[
            jax.ShapeDtypeStruct((n, GDN_WIDTH), F32),
            jax.ShapeDtypeStruct((n, GDN_WIDTH), BF16),
            jax.ShapeDtypeStruct((n, GDN_WIDTH), BF16),
            jax.ShapeDtypeStruct((n, GDN_WIDTH), BF16),
            jax.ShapeDtypeStruct((batch * GDN_HEADS * GDN_DK, seq), BF16),
            jax.ShapeDtypeStruct((n // GDN_BLOCK * 8, GDN_WIDTH), F32),
        ],
        compiler_params=_cparams(("parallel", "parallel", "parallel")),
        name="gdn_prep",
    )(proj, proj, proj, proj, proj, proj, conv_w, conv_w, conv_w, gb)


def _gdn_scan_kernel(u_ref, w_ref, qd_ref, ai_ref, kdt_ref, gl_ref, z_ref, g_ref, o_ref, s_ref, *, ts, hg):
    @pl.when(pl.program_id(2) == 0)
    def _():
        s_ref[...] = jnp.zeros_like(s_ref)

    blk = GDN_BLOCK
    gain = g_ref[...]
    for p in range(ts // blk):
        r = slice(p * blk, (p + 1) * blk)
        for hh in range(hg):
            c = slice(hh * LANES, (hh + 1) * LANES)
            state = s_ref[hh]
            lhs1 = jnp.concatenate([w_ref[r, c], qd_ref[r, c]], axis=0)
            r1 = jnp.dot(lhs1, state.astype(BF16), preferred_element_type=F32)
            v_new = u_ref[r, c] - r1[:blk]
            lhs2 = jnp.concatenate([ai_ref[r, c], kdt_ref[c, r]], axis=0)
            r2 = jnp.dot(lhs2, v_new.astype(BF16), preferred_element_type=F32)
            o = r1[blk:] + r2[:blk]
            gl = jnp.concatenate([gl_ref[p * 8:(p + 1) * 8, c]] * (blk // 8), axis=0)
            s_ref[hh] = state * gl + r2[blk:]
            z = z_ref[r, c].astype(F32)
            o_ref[r, c] = (_rms_rows(o, gain) * _silu(z)).astype(o_ref.dtype)


def _gdn_scan(u, w, qd, ai, kdt, gl8, proj, g_out, batch, seq, ts=512, hg=8):
    n = u.shape[0]
    gw = hg * LANES
    ns = seq // ts
    ngrp = GDN_HEADS // hg
    tok = lambda b, g, i: (b * ns + i, g)
    return pl.pallas_call(
        functools.partial(_gdn_scan_kernel, ts=ts, hg=hg),
        grid=(batch, ngrp, ns),
        in_specs=[
            pl.BlockSpec((ts, gw), tok),
            pl.BlockSpec((ts, gw), tok),
            pl.BlockSpec((ts, gw), tok),
            pl.BlockSpec((ts, gw), tok),
            pl.BlockSpec((gw, ts), lambda b, g, i: (b * ngrp + g, i)),
            pl.BlockSpec((ts // GDN_BLOCK * 8, gw), tok),
            pl.BlockSpec((ts, gw), lambda b, g, i: (b * ns + i, OFF_Z_GDN // gw + g)),
            pl.BlockSpec((1, LANES), lambda b, g, i: (0, 0)),
        ],
        out_specs=pl.BlockSpec((ts, gw), tok),
        out_shape=jax.ShapeDtypeStruct((n, GDN_WIDTH), BF16),
        scratch_shapes=[pltpu.VMEM((hg, GDN_DK, GDN_DV), F32)],
        compiler_params=_cparams(("parallel", "parallel", "arbitrary")),
        name="gdn_scan",
    )(u, w, qd, ai, kdt, gl8, proj, g_out)


def _merge_kernel(a1_ref, a2_ref, w1_ref, w2_ref, g1_ref, g2_ref, b_ref, h_ref):
    y1 = jnp.dot(a1_ref[...], w1_ref[...], preferred_element_type=F32)
    y2 = jnp.dot(a2_ref[...], w2_ref[...], preferred_element_type=F32)
    s1 = _sigmoid(g1_ref[...].astype(F32) + b_ref[0:1, :])
    s2 = _sigmoid(g2_ref[...].astype(F32) + b_ref[1:2, :])
    h_ref[...] = (s1 * y1 + s2 * y2).astype(h_ref.dtype)


def _merge(o_mla, o_gdn, w_o_mla, w_o_gdn, proj, b_gate, tm=1024, tn=1024):
    n = o_mla.shape[0]
    return pl.pallas_call(
        _merge_kernel,
        grid=(n // tm, D_MODEL // tn),
        in_specs=[
            pl.BlockSpec((tm, MLA_WIDTH), lambda i, j: (i, 0)),
            pl.BlockSpec((tm, GDN_WIDTH), lambda i, j: (i, 0)),
            pl.BlockSpec((MLA_WIDTH, tn), lambda i, j: (0, j)),
            pl.BlockSpec((GDN_WIDTH, tn), lambda i, j: (0, j)),
            pl.BlockSpec((tm, tn), lambda i, j: (i, OFF_GATE_MLA // tn + j)),
            pl.BlockSpec((tm, tn), lambda i, j: (i, OFF_GATE_GDN // tn + j)),
            pl.BlockSpec((2, tn), lambda i, j: (0, j)),
        ],
        out_specs=pl.BlockSpec((tm, tn), lambda i, j: (i, j)),
        out_shape=jax.ShapeDtypeStruct((n, D_MODEL), BF16),
        compiler_params=_cparams(("parallel", "parallel")),
        name="merge",
    )(o_mla, o_gdn, w_o_mla, w_o_gdn, proj, proj, b_gate)


def _out_kernel(h_ref, w_ref, x_ref, o_ref):
    o_ref[...] = x_ref[...] + jnp.dot(h_ref[...], w_ref[...], preferred_element_type=F32)


def _out_proj(h, w_out, x2, tm=1024, tn=1024):
    n = h.shape[0]
    return pl.pallas_call(
        _out_kernel,
        grid=(n // tm, D_MODEL // tn),
        in_specs=[
            pl.BlockSpec((tm, D_MODEL), lambda i, j: (i, 0)),
            pl.BlockSpec((D_MODEL, tn), lambda i, j: (0, j)),
            pl.BlockSpec((tm, tn), lambda i, j: (i, j)),
        ],
        out_specs=pl.BlockSpec((tm, tn), lambda i, j: (i, j)),
        out_shape=jax.ShapeDtypeStruct((n, D_MODEL), F32),
        compiler_params=_cparams(("parallel", "parallel")),
        name="out_proj",
    )(h, w_out, x2)


def _pad_cols(a, width):
    return jnp.pad(a, ((0, 0), (0, width - a.shape[1])))


def kernel(x, positions, w_norm, w_in, b_gate, w_q_up, w_kv_up, g_q_latent, g_kv_latent, g_q_head, g_k_head,
           w_o_mla, conv_w, a_log, dt_bias, g_gdn_out, w_o_gdn, w_out):
    batch, seq, _ = x.shape
    n = batch * seq
    x2 = x.reshape(n, D_MODEL)

    o_cq, o_ckv, o_kr = 0, Q_LORA, Q_LORA + KV_LORA
    o_zm = o_kr + QK_ROPE
    o_qkv = o_zm + MLA_WIDTH
    o_b = o_qkv + GDN_QKV
    o_a = o_b + GDN_HEADS
    o_zg = o_a + GDN_HEADS
    o_gm = o_zg + GDN_WIDTH
    o_gg = o_gm + D_MODEL
    w_t = w_in.T
    rows = lambda lo, count: w_t[lo:lo + count]
    w_in_pt = jnp.concatenate([
        rows(o_gm, D_MODEL), rows(o_gg, D_MODEL), rows(o_qkv, GDN_QKV), rows(o_zm, MLA_WIDTH), rows(o_zg, GDN_WIDTH),
        rows(o_cq, Q_LORA), rows(o_ckv, KV_LORA), rows(o_kr, QK_ROPE),
        jnp.zeros((PROJ_WIDTH - OFF_KROPE - QK_ROPE, D_MODEL), w_in.dtype)], axis=0).astype(BF16)

    ngrp = GDN_HEADS // GDN_GROUP

    def gate_rows(beta_part, decay_part):
        width = beta_part.shape[1]
        out = []
        for g in range(ngrp):
            hs = slice(g * GDN_GROUP, (g + 1) * GDN_GROUP)
            out += [beta_part[hs], jnp.zeros((16 - GDN_GROUP, width), F32),
                    decay_part[hs], jnp.zeros((LANES - 16 - GDN_GROUP, width), F32)]
        return jnp.concatenate(out, axis=0)

    w_bd_t = gate_rows(rows(o_b, GDN_HEADS), rows(o_a, GDN_HEADS)).astype(BF16)
    zeros_h = jnp.zeros((GDN_HEADS, 1), F32)
    par = jnp.concatenate([gate_rows(zeros_h, a_log[:, None]).T, gate_rows(zeros_h, dt_bias[:, None]).T,
                           jnp.zeros((6, ngrp * LANES), F32)], axis=0)

    w_q3 = w_q_up.reshape(Q_LORA, MLA_HEADS, QK_HEAD)
    w_q = jnp.concatenate([w_q3, w_q3[:, :, QK_NOPE:]], axis=2)
    w_q = w_q.reshape(Q_LORA, MLA_HEADS * QK_PAD).astype(BF16)
    w_kv3 = w_kv_up.reshape(KV_LORA, MLA_HEADS, QK_NOPE + V_HEAD)
    w_kn = w_kv3[:, :, :QK_NOPE].reshape(KV_LORA, MLA_HEADS * QK_NOPE).astype(BF16)
    w_vt = jnp.pad(w_kv3[:, :, QK_NOPE:], ((0, 0), (0, 0), (0, VT_ROWS - V_HEAD)))
    w_vt = w_vt.reshape(KV_LORA, MLA_HEADS * VT_ROWS).T.astype(BF16)
    g_qh = jnp.concatenate([g_q_head, g_q_head[QK_NOPE:]])[None, :]
    g_kn = g_k_head[None, :QK_NOPE]
    g_kr = _pad_cols(g_k_head[None, QK_NOPE:], LANES)
    inv_freq = ROPE_THETA ** (-jnp.arange(0, QK_ROPE, 2, dtype=F32) / QK_ROPE)
    invf = jnp.tile(inv_freq, LANES // (QK_ROPE // 2))[None, :]
    pos2 = positions.reshape(n, 1)

    xn, gb = _norm_gate(x2, w_norm[None, :], w_bd_t, par)
    proj = _matmul_nt(xn, w_in_pt, BF16, 1024, 1024, "in_proj")

    q, k, vt = _mla_prep(proj, pos2, invf, g_q_latent[None, :], g_kv_latent[None, :], w_q, w_kn, w_vt, g_qh, g_kn,
                         g_kr, batch, seq)
    o_mla = _attention(q, k, vt, proj, batch, seq)

    u, w, qd, ai, kdt, gl8 = _gdn_prep(proj, conv_w, gb, batch, seq)
    o_gdn = _gdn_scan(u, w, qd, ai, kdt, gl8, proj, g_gdn_out[None, :], batch, seq)

    h = _merge(o_mla, o_gdn, w_o_mla.astype(BF16), w_o_gdn.astype(BF16), proj, b_gate)
    out = _out_proj(h, w_out.astype(BF16), x2)
    return out.reshape(batch, seq, D_MODEL)
```

```python
import functools

import jax
import jax.numpy as jnp
from jax import lax
from jax.experimental import pallas as pl
from jax.experimental.pallas import tpu as pltpu

F32 = jnp.float32
BF16 = jnp.bfloat16

D_MODEL = 4096
CHUNK = 64
RMS_EPS = 1e-6
L2_EPS = 1e-6

MLA_HEADS = 16
QK_NOPE = 128
QK_ROPE = 64
QK_HEAD = QK_NOPE + QK_ROPE
V_HEAD = 128
Q_LORA = 1024
KV_LORA = 512
ROPE_THETA = 10000.0
MLA_WIDTH = MLA_HEADS * V_HEAD

GDN_HEADS = 16
GDN_DK = 128
GDN_DV = 128
GDN_KEY_WIDTH = GDN_HEADS * GDN_DK
GDN_WIDTH = GDN_HEADS * GDN_DV
GDN_QKV = 2 * GDN_KEY_WIDTH + GDN_WIDTH
CONV_K = 4

LANES = 128
QK_PAD = 256
GDN_BLOCK = 128
GDN_GROUP = 4

OFF_GATE_MLA = 0
OFF_GATE_GDN = OFF_GATE_MLA + D_MODEL
OFF_QKV = OFF_GATE_GDN + D_MODEL
OFF_Z_MLA = OFF_QKV + GDN_QKV
OFF_Z_GDN = OFF_Z_MLA + MLA_WIDTH
OFF_CQ = OFF_Z_GDN + GDN_WIDTH
OFF_CKV = OFF_CQ + Q_LORA
OFF_KROPE = OFF_CKV + KV_LORA
PROJ_WIDTH = 20480

VMEM_LIMIT = 56 * 1024 * 1024
LOG2E = 1.4426950408889634
VT_ROWS = 144


def _cparams(sem):
    return pltpu.CompilerParams(dimension_semantics=sem, vmem_limit_bytes=VMEM_LIMIT)


def _sigmoid(x):
    return 1.0 / (1.0 + jnp.exp(-x))


def _silu(x):
    h = 0.5 * x
    return h + h * jnp.tanh(h)


def _norm_gate_kernel(x_ref, wn_ref, wbd_ref, par_ref, xn_ref, gb_ref):
    x = x_ref[...]
    ms = jnp.mean(x * x, axis=-1, keepdims=True)
    xn = (x * lax.rsqrt(ms + RMS_EPS) * wn_ref[...]).astype(BF16)
    xn_ref[...] = xn
    bd = lax.dot_general(xn, wbd_ref[...], (((1,), (1,)), ((), ())), preferred_element_type=F32)
    neg_a = -jnp.exp(par_ref[0:1, :])
    z = bd + par_ref[1:2, :]
    softplus = jnp.maximum(z, 0.0) + jnp.log1p(jnp.exp(-jnp.abs(z)))
    lane = lax.broadcasted_iota(jnp.int32, bd.shape, 1) % LANES
    gb_ref[...] = jnp.where(lane < 16, _sigmoid(bd), neg_a * softplus)


def _norm_gate(x2, w_norm, w_bd_t, par, tm=512):
    n = x2.shape[0]
    gw = w_bd_t.shape[0]
    return pl.pallas_call(
        _norm_gate_kernel,
        grid=(n // tm,),
        in_specs=[
            pl.BlockSpec((tm, D_MODEL), lambda i: (i, 0)),
            pl.BlockSpec((1, D_MODEL), lambda i: (0, 0)),
            pl.BlockSpec((gw, D_MODEL), lambda i: (0, 0)),
            pl.BlockSpec((8, gw), lambda i: (0, 0)),
        ],
        out_specs=[
            pl.BlockSpec((tm, D_MODEL), lambda i: (i, 0)),
            pl.BlockSpec((tm, gw), lambda i: (i, 0)),
        ],
        out_shape=[
            jax.ShapeDtypeStruct((n, D_MODEL), BF16),
            jax.ShapeDtypeStruct((n, gw), F32),
        ],
        compiler_params=_cparams(("parallel",)),
        name="norm_gate",
    )(x2, w_norm, w_bd_t, par)


def _matmul_nt_kernel(a_ref, bt_ref, o_ref):
    o_ref[...] = lax.dot_general(a_ref[...], bt_ref[...], (((1,), (1,)), ((), ())),
                                 preferred_element_type=F32).astype(o_ref.dtype)


def _matmul_nt(a, b_t, out_dtype, tm, tn, name, src_rows):
    m, k = a.shape
    n_tiles = len(src_rows)
    align = 32
    assert all(r % align == 0 and r + tn <= b_t.shape[0] for r in src_rows)

    def first_row(j):
        unit = jnp.int32(src_rows[-1] // align)
        for t in range(n_tiles - 2, -1, -1):
            unit = jnp.where(j == t, jnp.int32(src_rows[t] // align), unit)
        return unit * align

    return pl.pallas_call(
        _matmul_nt_kernel,
        grid=(m // tm, n_tiles),
        in_specs=[
            pl.BlockSpec((tm, k), lambda i, j: (i, 0)),
            pl.BlockSpec((pl.Element(tn), pl.Element(k)), lambda i, j: (first_row(j), 0)),
        ],
        out_specs=pl.BlockSpec((tm, tn), lambda i, j: (i, j)),
        out_shape=jax.ShapeDtypeStruct((m, n_tiles * tn), out_dtype),
        compiler_params=_cparams(("parallel", "parallel")),
        name=name,
    )(a, b_t)


def _rope_tables(pos_ref, invf_ref):
    ang = pos_ref[...].astype(F32) * invf_ref[...]
    cos = jnp.cos(ang)
    sin = jnp.sin(ang)
    lane = lax.broadcasted_iota(jnp.int32, ang.shape, 1)
    c = jnp.where(lane < QK_ROPE, cos, 0.0)
    s_lo = jnp.where(lane < QK_ROPE // 2, -sin, 0.0)
    s_hi = jnp.where((lane >= QK_ROPE // 2) & (lane < QK_ROPE), sin, 0.0)
    return c, s_lo, s_hi


def _rope(r, tables):
    c, s_lo, s_hi = tables
    half = QK_ROPE // 2
    return r * c + pltpu.roll(r, LANES - half, 1) * s_lo + pltpu.roll(r, half, 1) * s_hi


def _rope_dup(r, tables):
    c, s_lo, s_hi = tables
    return r * c + pltpu.roll(r, QK_ROPE // 2, 1) * (s_lo + s_hi)


def _rms_rows(c, gain):
    ms = jnp.mean(c * c, axis=-1, keepdims=True)
    return c * lax.rsqrt(ms + RMS_EPS) * gain


def _mla_prep_kernel(cq_ref, ckv_ref, kr_ref, pos_ref, invf_ref, glq_ref, glkv_ref, wq_ref, wkn_ref, wvt_ref,
                     gqh_ref, gkn_ref, gkr_ref, q_ref, k_ref, vt_ref):
    heads = range(MLA_HEADS)
    tables = _rope_tables(pos_ref, invf_ref)
    cq = _rms_rows(cq_ref[...].astype(F32), glq_ref[...]).astype(BF16)
    ckv = _rms_rows(ckv_ref[...].astype(F32), glkv_ref[...]).astype(BF16)
    q = jnp.dot(cq, wq_ref[...], preferred_element_type=F32)
    k_nope = jnp.dot(ckv, wkn_ref[...], preferred_element_type=F32)
    vt = lax.dot_general(wvt_ref[...], ckv, (((1,), (1,)), ((), ())), preferred_element_type=F32)
    ones_row = lax.broadcasted_iota(jnp.int32, vt.shape, 0) % VT_ROWS == V_HEAD
    vt_ref[...] = jnp.where(ones_row, 1.0, vt).astype(BF16)
    kr_lanes = lax.broadcasted_iota(jnp.int32, (1, LANES), 1) < QK_ROPE
    kr = jnp.where(kr_lanes, kr_ref[...].astype(F32), 0.0)
    ss_r = jnp.sum(kr * kr, axis=-1, keepdims=True)
    k_rot = _rope(kr * gkr_ref[...], tables)
    g_qn = gqh_ref[:, :QK_NOPE]
    g_qr = gqh_ref[:, QK_NOPE:]
    g_kn = gkn_ref[...]
    scale = QK_HEAD ** -0.5 * LOG2E

    qa = [q[:, h * QK_PAD:h * QK_PAD + QK_NOPE] for h in heads]
    qr = [q[:, h * QK_PAD + QK_NOPE:(h + 1) * QK_PAD] for h in heads]
    kn = [k_nope[:, h * QK_NOPE:(h + 1) * QK_NOPE] for h in heads]
    first_copy = (lax.broadcasted_iota(jnp.int32, (1, LANES), 1) < QK_ROPE).astype(F32)
    qm = [r * first_copy for r in qr]
    ss_q = [jnp.sum(a * a + r * r, axis=-1, keepdims=True) for a, r in zip(qa, qm)]
    ss_k = [jnp.sum(x * x, axis=-1, keepdims=True) + ss_r for x in kn]
    inv_q = [lax.rsqrt(s * (1.0 / QK_HEAD) + RMS_EPS) * scale for s in ss_q]
    inv_k = [lax.rsqrt(s * (1.0 / QK_HEAD) + RMS_EPS) for s in ss_k]
    q_rot = [_rope_dup(r * g_qr, tables) for r in qr]
    for h in heads:
        q_ref[:, h * QK_PAD:h * QK_PAD + QK_NOPE] = (qa[h] * inv_q[h] * g_qn).astype(BF16)
        q_ref[:, h * QK_PAD + QK_NOPE:(h + 1) * QK_PAD] = (q_rot[h] * inv_q[h]).astype(BF16)
        k_ref[:, h * QK_PAD:h * QK_PAD + QK_NOPE] = (kn[h] * inv_k[h] * g_kn).astype(BF16)
        k_ref[:, h * QK_PAD + QK_NOPE:(h + 1) * QK_PAD] = (k_rot * inv_k[h]).astype(BF16)


def _mla_prep(proj, pos2, invf, g_q_lat, g_kv_lat, w_q, w_kn, w_vt, g_qh, g_kn, g_kr, batch, seq, tm=256):
    n = proj.shape[0]
    width = MLA_HEADS * QK_PAD
    ns = seq // tm
    tok = lambda col: (lambda b, i: (b * ns + i, col))
    const = lambda b, i: (0, 0)
    return pl.pallas_call(
        _mla_prep_kernel,
        grid=(batch, ns),
        in_specs=[
            pl.BlockSpec((tm, Q_LORA), tok(OFF_CQ // Q_LORA)),
            pl.BlockSpec((tm, KV_LORA), tok(OFF_CKV // KV_LORA)),
            pl.BlockSpec((tm, LANES), tok(OFF_KROPE // LANES)),
            pl.BlockSpec((tm, 1), tok(0)),
            pl.BlockSpec((1, LANES), const),
            pl.BlockSpec((1, Q_LORA), const),
            pl.BlockSpec((1, KV_LORA), const),
            pl.BlockSpec((Q_LORA, width), const),
            pl.BlockSpec((KV_LORA, MLA_HEADS * QK_NOPE), const),
            pl.BlockSpec((MLA_HEADS * VT_ROWS, KV_LORA), const),
            pl.BlockSpec((1, QK_PAD), const),
            pl.BlockSpec((1, LANES), const),
            pl.BlockSpec((1, LANES), const),
        ],
        out_specs=[
            pl.BlockSpec((tm, width), tok(0)),
            pl.BlockSpec((tm, width), tok(0)),
            pl.BlockSpec((MLA_HEADS * VT_ROWS, tm), lambda b, i: (b, i)),
        ],
        out_shape=[
            jax.ShapeDtypeStruct((n, width), BF16),
            jax.ShapeDtypeStruct((n, width), BF16),
            jax.ShapeDtypeStruct((batch * MLA_HEADS * VT_ROWS, seq), BF16),
        ],
        compiler_params=_cparams(("parallel", "parallel")),
        name="mla_prep",
    )(proj, proj, proj, pos2, invf, g_q_lat, g_kv_lat, w_q, w_kn, w_vt, g_qh, g_kn, g_kr)


def _attn_kernel(q_ref, k_ref, vt_ref, z_ref, o_ref, *, seq, tq):
    nq = seq // tq
    hq = tq // 2

    def chunk_mask(n_keys, query_offset):
        key_chunk = lax.broadcasted_iota(jnp.int32, (n_keys, hq), 0) // CHUNK
        qry_chunk = (lax.broadcasted_iota(jnp.int32, (n_keys, hq), 1) + query_offset) // CHUNK
        return key_chunk <= qry_chunk

    diag_mask_x = chunk_mask(hq, 0)
    diag_mask_y = chunk_mask(tq, hq)

    def qk(q, k):
        return lax.dot_general(k, q, (((1,), (1,)), ((), ())), preferred_element_type=F32)

    def softmax_step(st, carry, mask):
        m, acc = carry
        if mask is not None:
            st = jnp.where(mask, st, -jnp.inf)
        m_new = jnp.maximum(m, jnp.max(st, axis=0, keepdims=True))
        alpha = jnp.exp2(m - m_new)
        p = jnp.exp2(st - m_new)
        return p.astype(BF16), alpha, (m_new, acc)

    def pv(vt, p, alpha, carry):
        m, acc = carry
        return m, alpha * acc + jnp.dot(vt, p, preferred_element_type=F32)

    def full_tile(off):
        half = (k_ref[off:off + tq, :], vt_ref[:, off:off + tq], None)
        return half, half

    def diag_tile(off):
        return ((k_ref[off:off + hq, :], vt_ref[:, off:off + hq], diag_mask_x),
                (k_ref[off:off + tq, :], vt_ref[:, off:off + tq], diag_mask_y))

    steps = [(i, t) for i in range(nq) for t in range(i + 1)]
    qs, carries, tiles, scores, probs = {}, {}, {}, {}, {}

    def emit_qk(s):
        i, t = steps[s]
        if t == 0:
            qs[i] = (q_ref[i * tq:i * tq + hq, :], q_ref[i * tq + hq:(i + 1) * tq, :])
            init = (jnp.full((1, hq), -1e30, F32), jnp.zeros((VT_ROWS, hq), F32))
            carries[i] = [init, init]
        tiles[s] = diag_tile(i * tq) if t == i else full_tile(t * tq)
        scores[s] = [qk(qs[i][x], tiles[s][x][0]) for x in range(2)]

    def emit_softmax(s):
        i, _ = steps[s]
        st = scores.pop(s)
        out = []
        for x in range(2):
            p, alpha, carries[i][x] = softmax_step(st[x], carries[i][x], tiles[s][x][2])
            out.append((p, alpha))
        probs[s] = out

    def emit_pv(s):
        i, t = steps[s]
        tile = tiles.pop(s)
        for x, (p, alpha) in enumerate(probs.pop(s)):
            carries[i][x] = pv(tile[x][1], p, alpha, carries[i][x])
        if t == i:
            for x, (_, acc) in enumerate(carries.pop(i)):
                rows = slice(i * tq + x * hq, i * tq + (x + 1) * hq)
                o = (acc[:V_HEAD] / acc[V_HEAD:V_HEAD + 1]).T
                o_ref[rows, :] = (o * _silu(z_ref[rows, :].astype(F32))).astype(o_ref.dtype)

    emit_qk(0)
    for s in range(len(steps)):
        if s + 1 < len(steps):
            emit_qk(s + 1)
        emit_softmax(s)
        if s > 0:
            emit_pv(s - 1)
    emit_pv(len(steps) - 1)


def _attention(q, k, vt, proj, batch, seq, tq=512):
    n = q.shape[0]
    return pl.pallas_call(
        functools.partial(_attn_kernel, seq=seq, tq=tq),
        grid=(batch, MLA_HEADS),
        in_specs=[
            pl.BlockSpec((seq, QK_PAD), lambda b, h: (b, h)),
            pl.BlockSpec((seq, QK_PAD), lambda b, h: (b, h)),
            pl.BlockSpec((VT_ROWS, seq), lambda b, h: (b * MLA_HEADS + h, 0)),
            pl.BlockSpec((seq, V_HEAD), lambda b, h: (b, OFF_Z_MLA // V_HEAD + h)),
        ],
        out_specs=pl.BlockSpec((seq, V_HEAD), lambda b, h: (b, h)),
        out_shape=jax.ShapeDtypeStruct((n, MLA_WIDTH), BF16),
        compiler_params=_cparams(("parallel", "parallel")),
        name="mla_attention",
    )(q, k, vt, proj)


def _gdn_prep_kernel(q_ref, k_ref, v_ref, qh_ref, kh_ref, vh_ref, cwq_ref, cwk_ref, cwv_ref, gb_ref,
                     u_ref, w_ref, qd_ref, ai_ref, kdt_ref, gl_ref, *, tt):
    first = pl.program_id(1) == 0
    blk = GDN_BLOCK
    nb = tt // blk
    pair = 2 * LANES

    gb = gb_ref[...]
    rmod = lax.broadcasted_iota(jnp.int32, (tt, LANES), 0) % blk
    gc = gb
    s = 1
    while s < blk:
        gc = gc + jnp.where(rmod >= s, pltpu.roll(gc, s, 0), 0.0)
        s *= 2
    gc_t = [gc[p * blk:(p + 1) * blk].T for p in range(nb)]

    halo_rows = 16
    win = blk + halo_rows
    srow = lax.broadcasted_iota(jnp.int32, ((CONV_K - 1) * blk, win), 0)
    scol = lax.broadcasted_iota(jnp.int32, ((CONV_K - 1) * blk, win), 1)
    shift_mat = (scol == srow % blk + halo_rows - 1 - srow // blk).astype(BF16)

    def shifted(x_ref, h_ref):
        halo = jnp.where(first, jnp.zeros_like(h_ref[...]), h_ref[...])
        xe = jnp.concatenate([halo, x_ref[...]], axis=0)
        return [jnp.dot(shift_mat, xe[p * blk:p * blk + win], preferred_element_type=F32) for p in range(nb)]

    def conv_silu(x_ref, cw_ref, sh):
        cw = cw_ref[...]
        out = []
        for p in range(nb):
            y = x_ref[p * blk:(p + 1) * blk, :].astype(F32) * cw[CONV_K - 1:CONV_K]
            for s in range(CONV_K - 1):
                y = y + sh[p][s * blk:(s + 1) * blk] * cw[CONV_K - 2 - s:CONV_K - 1 - s]
            out.append(_silu(y))
        return jnp.concatenate(out, axis=0)

    def l2norm_heads(x, scale):
        parts = []
        for hh in range(GDN_GROUP):
            xs = x[:, hh * LANES:(hh + 1) * LANES]
            parts.append(xs * (lax.rsqrt(jnp.sum(xs * xs, axis=-1, keepdims=True) + L2_EPS) * scale))
        return jnp.concatenate(parts, axis=1)

    def lane_bcast(x, lane0):
        return jnp.concatenate(
            [jnp.broadcast_to(x[:, lane0 + hh:lane0 + hh + 1], (tt, LANES)) for hh in range(GDN_GROUP)], axis=1)

    sh_q = shifted(q_ref, qh_ref)
    sh_k = shifted(k_ref, kh_ref)
    q = l2norm_heads(conv_silu(q_ref, cwq_ref, sh_q), GDN_DK ** -0.5)
    sh_v = shifted(v_ref, vh_ref)
    k = l2norm_heads(conv_silu(k_ref, cwk_ref, sh_k), 1.0)
    v = conv_silu(v_ref, cwv_ref, sh_v)
    beta_b = lane_bcast(gb, 0)
    gcol_b = lane_bcast(gc, 16)
    egc = jnp.exp(gcol_b)
    kb = k * beta_b
    vb = v * beta_b
    kbe = kb * egc
    qd_ref[...] = (q * egc).astype(BF16)
    glast_b = jnp.concatenate(
        [jnp.broadcast_to(gcol_b[(p + 1) * blk - 1:(p + 1) * blk, :], (blk, gcol_b.shape[1])) for p in range(nb)], axis=0)
    kd = k * jnp.exp(glast_b - gcol_b)
    for p in range(nb):
        r = slice(p * blk, (p + 1) * blk)
        gl_ref[p * 8:(p + 1) * 8, :] = jnp.exp(glast_b[p * blk:p * blk + 8, :])
        for hh in range(GDN_GROUP):
            c = slice(hh * LANES, (hh + 1) * LANES)
            kdt_ref[c, r] = kd[r, c].T.astype(BF16)

    ri = lax.broadcasted_iota(jnp.int32, (blk, pair), 0)
    ci = lax.broadcasted_iota(jnp.int32, (blk, pair), 1) % LANES
    causal = ri >= ci
    strict = ri > ci
    zero = jnp.zeros((blk, LANES), BF16)

    def bdiag(x):
        return jnp.concatenate([jnp.concatenate([x[:, :LANES], zero], axis=1),
                                jnp.concatenate([zero, x[:, LANES:]], axis=1)], axis=0)

    def mm(a, b):
        return jnp.dot(a, b, preferred_element_type=F32)

    items = [(p, j) for p in range(nb) for j in range(GDN_GROUP // 2)]
    rc = lambda p, j: (slice(p * blk, (p + 1) * blk), slice(j * pair, (j + 1) * pair))

    kq = []
    for p, j in items:
        r, c = rc(p, j)
        lhs = jnp.concatenate([kb[r, c].astype(BF16), q[r, c].astype(BF16)], axis=0)
        kq.append(lax.dot_general(lhs, bdiag(k[r, c].astype(BF16)), (((1,), (1,)), ((), ())),
                                  preferred_element_type=F32))
    e = []
    for (p, j), kq_i in zip(items, kq):
        r, c = rc(p, j)
        grow = jnp.concatenate([gc_t[p][16 + 2 * j:17 + 2 * j, :], gc_t[p][17 + 2 * j:18 + 2 * j, :]], axis=1)
        diff = gcol_b[r, c] - grow
        decay = jnp.where(causal, jnp.exp(jnp.where(causal, diff, 0.0)), 0.0)
        ai_ref[r, c] = (kq_i[blk:] * decay).astype(BF16)
        e.append(jnp.where(strict, -(kq_i[:blk] * decay), 0.0))
    eb = [x.astype(BF16) for x in e]
    pw = [mm(x, bdiag(x)) for x in eb]
    span = 2
    while span * 2 < blk:
        pwb = [x.astype(BF16) for x in pw]
        both = [mm(jnp.concatenate([e_i.astype(BF16), p_i], axis=0), bdiag(p_i)) for e_i, p_i in zip(e, pwb)]
        e = [e_i + p_i + b_i[:blk] for e_i, p_i, b_i in zip(e, pw, both)]
        pw = [b_i[blk:] for b_i in both]
        span *= 2
    last = [mm(e_i.astype(BF16), bdiag(p_i.astype(BF16))) for e_i, p_i in zip(e, pw)]
    e = [e_i + p_i + l_i for e_i, p_i, l_i in zip(e, pw, last)]
    for (p, j), e_i in zip(items, e):
        r, _ = rc(p, j)
        for s in range(2):
            c = slice((2 * j + s) * LANES, (2 * j + s + 1) * LANES)
            rhs = jnp.concatenate([vb[r, c].astype(BF16), kbe[r, c].astype(BF16)], axis=1)
            sol = mm(e_i[:, s * LANES:(s + 1) * LANES].astype(BF16), rhs)
            u_ref[r, c] = vb[r, c] + sol[:, :LANES]
            w_ref[r, c] = (kbe[r, c] + sol[:, LANES:]).astype(BF16)


def _gdn_prep(proj, conv_w, gb, batch, seq, tt=1024):
    n = proj.shape[0]
    gw = GDN_GROUP * LANES
    ns = seq // tt
    ngrp = GDN_HEADS // GDN_GROUP
    col = lambda base: (lambda b, i, g: (b * ns + i, base // gw + g))
    halo = lambda base: (lambda b, i, g: (jnp.maximum((b * ns + i) * (tt // 16) - 1, 0), base // gw + g))
    cwc = lambda base: (lambda b, i, g: (0, base // gw + g))
    tok = lambda b, i, g: (b * ns + i, g)
    return pl.pallas_call(
        functools.partial(_gdn_prep_kernel, tt=tt),
        grid=(batch, ns, ngrp),
        in_specs=[
            pl.BlockSpec((tt, gw), col(OFF_QKV)),
            pl.BlockSpec((tt, gw), col(OFF_QKV + GDN_KEY_WIDTH)),
            pl.BlockSpec((tt, gw), col(OFF_QKV + 2 * GDN_KEY_WIDTH)),
            pl.BlockSpec((16, gw), halo(OFF_QKV)),
            pl.BlockSpec((16, gw), halo(OFF_QKV + GDN_KEY_WIDTH)),
            pl.BlockSpec((16, gw), halo(OFF_QKV + 2 * GDN_KEY_WIDTH)),
            pl.BlockSpec((CONV_K, gw), cwc(0)),
            pl.BlockSpec((CONV_K, gw), cwc(GDN_KEY_WIDTH)),
            pl.BlockSpec((CONV_K, gw), cwc(2 * GDN_KEY_WIDTH)),
            pl.BlockSpec((tt, LANES), tok),
        ],
        out_specs=[
            pl.BlockSpec((tt, gw), tok),
            pl.BlockSpec((tt, gw), tok),
            pl.BlockSpec((tt, gw), tok),
            pl.BlockSpec((tt, gw), tok),
            pl.BlockSpec((gw, tt), lambda b, i, g: (b * ngrp + g, i)),
            pl.BlockSpec((tt // GDN_BLOCK * 8, gw), tok),
        ],
        out_shape=[
            jax.ShapeDtypeStruct((n, GDN_WIDTH), F32),
            jax.ShapeDtypeStruct((n, GDN_WIDTH), BF16),
            jax.ShapeDtypeStruct((n, GDN_WIDTH), BF16),
            jax.ShapeDtypeStruct((n, GDN_WIDTH), BF16),
            jax.ShapeDtypeStruct((batch * GDN_HEADS * GDN_DK, seq), BF16),
            jax.ShapeDtypeStruct((n // GDN_BLOCK * 8, GDN_WIDTH), F32),
        ],
        compiler_params=_cparams(("parallel", "parallel", "parallel")),
        name="gdn_prep",
    )(proj, proj, proj, proj, proj, proj, conv_w, conv_w, conv_w, gb)


def _gdn_scan_kernel(u_ref, w_ref, qd_ref, ai_ref, kdt_ref, gl_ref, z_ref, g_ref, o_ref, s_ref, *, ts, hg):
    @pl.when(pl.program_id(2) == 0)
    def _():
        s_ref[...] = jnp.zeros_like(s_ref)

    blk = GDN_BLOCK
    gain = g_ref[...]
    for p in range(ts // blk):
        r = slice(p * blk, (p + 1) * blk)
        for hh in range(hg):
            c = slice(hh * LANES, (hh + 1) * LANES)
            state = s_ref[hh]
            lhs1 = jnp.concatenate([w_ref[r, c], qd_ref[r, c]], axis=0)
            r1 = jnp.dot(lhs1, state.astype(BF16), preferred_element_type=F32)
            v_new = u_ref[r, c] - r1[:blk]
            lhs2 = jnp.concatenate([ai_ref[r, c], kdt_ref[c, r]], axis=0)
            r2 = jnp.dot(lhs2, v_new.astype(BF16), preferred_element_type=F32)
            o = r1[blk:] + r2[:blk]
            gl = jnp.concatenate([gl_ref[p * 8:(p + 1) * 8, c]] * (blk // 8), axis=0)
            s_ref[hh] = state * gl + r2[blk:]
            z = z_ref[r, c].astype(F32)
            o_ref[r, c] = (_rms_rows(o, gain) * _silu(z)).astype(o_ref.dtype)


def _gdn_scan(u, w, qd, ai, kdt, gl8, proj, g_out, batch, seq, ts=512, hg=8):
    n = u.shape[0]
    gw = hg * LANES
    ns = seq // ts
    ngrp = GDN_HEADS // hg
    tok = lambda b, g, i: (b * ns + i, g)
    return pl.pallas_call(
        functools.partial(_gdn_scan_kernel, ts=ts, hg=hg),
        grid=(batch, ngrp, ns),
        in_specs=[
            pl.BlockSpec((ts, gw), tok),
            pl.BlockSpec((ts, gw), tok),
            pl.BlockSpec((ts, gw), tok),
            pl.BlockSpec((ts, gw), tok),
            pl.BlockSpec((gw, ts), lambda b, g, i: (b * ngrp + g, i)),
            pl.BlockSpec((ts // GDN_BLOCK * 8, gw), tok),
            pl.BlockSpec((ts, gw), lambda b, g, i: (b * ns + i, OFF_Z_GDN // gw + g)),
            pl.BlockSpec((1, LANES), lambda b, g, i: (0, 0)),
        ],
        out_specs=pl.BlockSpec((ts, gw), tok),
        out_shape=jax.ShapeDtypeStruct((n, GDN_WIDTH), BF16),
        scratch_shapes=[pltpu.VMEM((hg, GDN_DK, GDN_DV), F32)],
        compiler_params=_cparams(("parallel", "parallel", "arbitrary")),
        name="gdn_scan",
    )(u, w, qd, ai, kdt, gl8, proj, g_out)


def _merge_kernel(a1_ref, a2_ref, w1_ref, w2_ref, g1_ref, g2_ref, b_ref, h_ref):
    y1 = jnp.dot(a1_ref[...], w1_ref[...], preferred_element_type=F32)
    y2 = jnp.dot(a2_ref[...], w2_ref[...], preferred_element_type=F32)
    s1 = _sigmoid(g1_ref[...].astype(F32) + b_ref[0:1, :])
    s2 = _sigmoid(g2_ref[...].astype(F32) + b_ref[1:2, :])
    h_ref[...] = (s1 * y1 + s2 * y2).astype(h_ref.dtype)


def _merge(o_mla, o_gdn, w_o_mla, w_o_gdn, proj, b_gate, tm=1024, tn=1024):
    n = o_mla.shape[0]
    return pl.pallas_call(
        _merge_kernel,
        grid=(n // tm, D_MODEL // tn),
        in_specs=[
            pl.BlockSpec((tm, MLA_WIDTH), lambda i, j: (i, 0)),
            pl.BlockSpec((tm, GDN_WIDTH), lambda i, j: (i, 0)),
            pl.BlockSpec((MLA_WIDTH, tn), lambda i, j: (0, j)),
            pl.BlockSpec((GDN_WIDTH, tn), lambda i, j: (0, j)),
            pl.BlockSpec((tm, tn), lambda i, j: (i, OFF_GATE_MLA // tn + j)),
            pl.BlockSpec((tm, tn), lambda i, j: (i, OFF_GATE_GDN // tn + j)),
            pl.BlockSpec((2, tn), lambda i, j: (0, j)),
        ],
        out_specs=pl.BlockSpec((tm, tn), lambda i, j: (i, j)),
        out_shape=jax.ShapeDtypeStruct((n, D_MODEL), BF16),
        compiler_params=_cparams(("parallel", "parallel")),
        name="merge",
    )(o_mla, o_gdn, w_o_mla, w_o_gdn, proj, proj, b_gate)


def _out_kernel(h_ref, w_ref, x_ref, o_ref):
    o_ref[...] = x_ref[...] + jnp.dot(h_ref[...], w_ref[...], preferred_element_type=F32)


def _out_proj(h, w_out, x2, tm=1024, tn=1024):
    n = h.shape[0]
    return pl.pallas_call(
        _out_kernel,
        grid=(n // tm, D_MODEL // tn),
        in_specs=[
            pl.BlockSpec((tm, D_MODEL), lambda i, j: (i, 0)),
            pl.BlockSpec((D_MODEL, tn), lambda i, j: (0, j)),
            pl.BlockSpec((tm, tn), lambda i, j: (i, j)),
        ],
        out_specs=pl.BlockSpec((tm, tn), lambda i, j: (i, j)),
        out_shape=jax.ShapeDtypeStruct((n, D_MODEL), F32),
        compiler_params=_cparams(("parallel", "parallel")),
        name="out_proj",
    )(h, w_out, x2)


def _pad_cols(a, width):
    return jnp.pad(a, ((0, 0), (0, width - a.shape[1])))


def kernel(x, positions, w_norm, w_in, b_gate, w_q_up, w_kv_up, g_q_latent, g_kv_latent, g_q_head, g_k_head,
           w_o_mla, conv_w, a_log, dt_bias, g_gdn_out, w_o_gdn, w_out):
    batch, seq, _ = x.shape
    n = batch * seq
    x2 = x.reshape(n, D_MODEL)

    o_cq, o_ckv, o_kr = 0, Q_LORA, Q_LORA + KV_LORA
    o_zm = o_kr + QK_ROPE
    o_qkv = o_zm + MLA_WIDTH
    o_b = o_qkv + GDN_QKV
    o_a = o_b + GDN_HEADS
    o_zg = o_a + GDN_HEADS
    o_gm = o_zg + GDN_WIDTH
    o_gg = o_gm + D_MODEL
    w_t = w_in.T
    rows = lambda lo, count: w_t[lo:lo + count]
    w_t_bf = w_t.astype(BF16)
    tile = 1024
    segments = ((o_gm, D_MODEL), (o_gg, D_MODEL), (o_qkv, GDN_QKV), (o_zm, MLA_WIDTH), (o_zg, GDN_WIDTH),
                (o_cq, Q_LORA), (o_ckv, tile))
    src_rows = tuple(lo + t * tile for lo, width in segments for t in range(width // tile))
    assert len(src_rows) * tile == PROJ_WIDTH

    ngrp = GDN_HEADS // GDN_GROUP

    def gate_rows(beta_part, decay_part):
        width = beta_part.shape[1]
        out = []
        for g in range(ngrp):
            hs = slice(g * GDN_GROUP, (g + 1) * GDN_GROUP)
            out += [beta_part[hs], jnp.zeros((16 - GDN_GROUP, width), F32),
                    decay_part[hs], jnp.zeros((LANES - 16 - GDN_GROUP, width), F32)]
        return jnp.concatenate(out, axis=0)

    w_bd_t = gate_rows(rows(o_b, GDN_HEADS), rows(o_a, GDN_HEADS)).astype(BF16)
    zeros_h = jnp.zeros((GDN_HEADS, 1), F32)
    par = jnp.concatenate([gate_rows(zeros_h, a_log[:, None]).T, gate_rows(zeros_h, dt_bias[:, None]).T,
                           jnp.zeros((6, ngrp * LANES), F32)], axis=0)

    w_q3 = w_q_up.reshape(Q_LORA, MLA_HEADS, QK_HEAD)
    w_q = jnp.concatenate([w_q3, w_q3[:, :, QK_NOPE:]], axis=2)
    w_q = w_q.reshape(Q_LORA, MLA_HEADS * QK_PAD).astype(BF16)
    w_kv3 = w_kv_up.reshape(KV_LORA, MLA_HEADS, QK_NOPE + V_HEAD)
    w_kn = w_kv3[:, :, :QK_NOPE].reshape(KV_LORA, MLA_HEADS * QK_NOPE).astype(BF16)
    w_vt = jnp.pad(w_kv3[:, :, QK_NOPE:], ((0, 0), (0, 0), (0, VT_ROWS - V_HEAD)))
    w_vt = w_vt.reshape(KV_LORA, MLA_HEADS * VT_ROWS).T.astype(BF16)
    g_qh = jnp.concatenate([g_q_head, g_q_head[QK_NOPE:]])[None, :]
    g_kn = g_k_head[None, :QK_NOPE]
    g_kr = _pad_cols(g_k_head[None, QK_NOPE:], LANES)
    inv_freq = ROPE_THETA ** (-jnp.arange(0, QK_ROPE, 2, dtype=F32) / QK_ROPE)
    invf = jnp.tile(inv_freq, LANES // (QK_ROPE // 2))[None, :]
    pos2 = positions.reshape(n, 1)

    xn, gb = _norm_gate(x2, w_norm[None, :], w_bd_t, par)
    proj = _matmul_nt(xn, w_t_bf, BF16, 1024, tile, "in_proj", src_rows)

    q, k, vt = _mla_prep(proj, pos2, invf, g_q_latent[None, :], g_kv_latent[None, :], w_q, w_kn, w_vt, g_qh, g_kn,
                         g_kr, batch, seq)
    o_mla = _attention(q, k, vt, proj, batch, seq)

    u, w, qd, ai, kdt, gl8 = _gdn_prep(proj, conv_w, gb, batch, seq)
    o_gdn = _gdn_scan(u, w, qd, ai, kdt, gl8, proj, g_gdn_out[None, :], batch, seq)

    h = _merge(o_mla, o_gdn, w_o_mla.astype(BF16), w_o_gdn.astype(BF16), proj, b_gate)
    out = _out_proj(h, w_out.astype(BF16), x2)
    return out.reshape(batch, seq, D_MODEL)
```

```python
import functools

import jax
import jax.numpy as jnp
from jax import lax
from jax.experimental import pallas as pl
from jax.experimental.pallas import tpu as pltpu

F32 = jnp.float32
BF16 = jnp.bfloat16

D_MODEL = 4096
CHUNK = 64
RMS_EPS = 1e-6
L2_EPS = 1e-6

MLA_HEADS = 16
QK_NOPE = 128
QK_ROPE = 64
QK_HEAD = QK_NOPE + QK_ROPE
V_HEAD = 128
Q_LORA = 1024
KV_LORA = 512
ROPE_THETA = 10000.0
MLA_WIDTH = MLA_HEADS * V_HEAD

GDN_HEADS = 16
GDN_DK = 128
GDN_DV = 128
GDN_KEY_WIDTH = GDN_HEADS * GDN_DK
GDN_WIDTH = GDN_HEADS * GDN_DV
GDN_QKV = 2 * GDN_KEY_WIDTH + GDN_WIDTH
CONV_K = 4

LANES = 128
QK_PAD = 256
GDN_BLOCK = 128
GDN_GROUP = 4

OFF_GATE_MLA = 0
OFF_GATE_GDN = OFF_GATE_MLA + D_MODEL
OFF_QKV = OFF_GATE_GDN + D_MODEL
OFF_Z_MLA = OFF_QKV + GDN_QKV
OFF_Z_GDN = OFF_Z_MLA + MLA_WIDTH
OFF_CQ = OFF_Z_GDN + GDN_WIDTH
OFF_CKV = OFF_CQ + Q_LORA
OFF_KROPE = OFF_CKV + KV_LORA
PROJ_WIDTH = 20480

VMEM_LIMIT = 56 * 1024 * 1024
LOG2E = 1.4426950408889634
VT_ROWS = 144


def _cparams(sem):
    return pltpu.CompilerParams(dimension_semantics=sem, vmem_limit_bytes=VMEM_LIMIT)


def _sigmoid(x):
    return 1.0 / (1.0 + jnp.exp(-x))


def _silu(x):
    h = 0.5 * x
    return h + h * jnp.tanh(h)


def _norm_gate_kernel(x_ref, wn_ref, wbd_ref, par_ref, xn_ref, gb_ref):
    x = x_ref[...]
    ms = jnp.mean(x * x, axis=-1, keepdims=True)
    xn = (x * lax.rsqrt(ms + RMS_EPS) * wn_ref[...]).astype(BF16)
    xn_ref[...] = xn
    bd = lax.dot_general(xn, wbd_ref[...], (((1,), (1,)), ((), ())), preferred_element_type=F32)
    neg_a = -jnp.exp(par_ref[0:1, :])
    z = bd + par_ref[1:2, :]
    softplus = jnp.maximum(z, 0.0) + jnp.log1p(jnp.exp(-jnp.abs(z)))
    lane = lax.broadcasted_iota(jnp.int32, bd.shape, 1) % LANES
    gb_ref[...] = jnp.where(lane < 16, _sigmoid(bd), neg_a * softplus)


def _norm_gate(x2, w_norm, w_bd_t, par, tm=512):
    n = x2.shape[0]
    gw = w_bd_t.shape[0]
    return pl.pallas_call(
        _norm_gate_kernel,
        grid=(n // tm,),
        in_specs=[
            pl.BlockSpec((tm, D_MODEL), lambda i: (i, 0)),
            pl.BlockSpec((1, D_MODEL), lambda i: (0, 0)),
            pl.BlockSpec((gw, D_MODEL), lambda i: (0, 0)),
            pl.BlockSpec((8, gw), lambda i: (0, 0)),
        ],
        out_specs=[
            pl.BlockSpec((tm, D_MODEL), lambda i: (i, 0)),
            pl.BlockSpec((tm, gw), lambda i: (i, 0)),
        ],
        out_shape=[
            jax.ShapeDtypeStruct((n, D_MODEL), BF16),
            jax.ShapeDtypeStruct((n, gw), F32),
        ],
        compiler_params=_cparams(("parallel",)),
        name="norm_gate",
    )(x2, w_norm, w_bd_t, par)


def _matmul_nt_kernel(a_ref, bt_ref, o_ref):
    o_ref[...] = lax.dot_general(a_ref[...], bt_ref[...], (((1,), (1,)), ((), ())),
                                 preferred_element_type=F32).astype(o_ref.dtype)


def _matmul_nt(a, b_t, out_dtype, tm, tn, name, src_rows):
    m, k = a.shape
    n_tiles = len(src_rows)
    align = 32
    assert all(r % align == 0 and r + tn <= b_t.shape[0] for r in src_rows)

    def first_row(j):
        unit = jnp.int32(src_rows[-1] // align)
        for t in range(n_tiles - 2, -1, -1):
            unit = jnp.where(j == t, jnp.int32(src_rows[t] // align), unit)
        return unit * align

    return pl.pallas_call(
        _matmul_nt_kernel,
        grid=(m // tm, n_tiles),
        in_specs=[
            pl.BlockSpec((tm, k), lambda i, j: (i, 0)),
            pl.BlockSpec((pl.Element(tn), pl.Element(k)), lambda i, j: (first_row(j), 0)),
        ],
        out_specs=pl.BlockSpec((tm, tn), lambda i, j: (i, j)),
        out_shape=jax.ShapeDtypeStruct((m, n_tiles * tn), out_dtype),
        compiler_params=_cparams(("parallel", "parallel")),
        name=name,
    )(a, b_t)


def _rope_tables(pos_ref, invf_ref):
    ang = pos_ref[...].astype(F32) * invf_ref[...]
    cos = jnp.cos(ang)
    sin = jnp.sin(ang)
    lane = lax.broadcasted_iota(jnp.int32, ang.shape, 1)
    c = jnp.where(lane < QK_ROPE, cos, 0.0)
    s_lo = jnp.where(lane < QK_ROPE // 2, -sin, 0.0)
    s_hi = jnp.where((lane >= QK_ROPE // 2) & (lane < QK_ROPE), sin, 0.0)
    return c, s_lo, s_hi


def _rope(r, tables):
    c, s_lo, s_hi = tables
    half = QK_ROPE // 2
    return r * c + pltpu.roll(r, LANES - half, 1) * s_lo + pltpu.roll(r, half, 1) * s_hi


def _rope_dup(r, tables):
    c, s_lo, s_hi = tables
    return r * c + pltpu.roll(r, QK_ROPE // 2, 1) * (s_lo + s_hi)


def _rms_rows(c, gain):
    ms = jnp.mean(c * c, axis=-1, keepdims=True)
    return c * lax.rsqrt(ms + RMS_EPS) * gain


def _mla_prep_kernel(cq_ref, ckv_ref, kr_ref, pos_ref, invf_ref, glq_ref, glkv_ref, wq_ref, wkn_ref, wvt_ref,
                     gqh_ref, gkn_ref, gkr_ref, q_ref, k_ref, vt_ref):
    heads = range(MLA_HEADS)
    tables = _rope_tables(pos_ref, invf_ref)
    cq = _rms_rows(cq_ref[...].astype(F32), glq_ref[...]).astype(BF16)
    ckv = _rms_rows(ckv_ref[...].astype(F32), glkv_ref[...]).astype(BF16)
    q = jnp.dot(cq, wq_ref[...], preferred_element_type=F32)
    k_nope = jnp.dot(ckv, wkn_ref[...], preferred_element_type=F32)
    vt = lax.dot_general(wvt_ref[...], ckv, (((1,), (1,)), ((), ())), preferred_element_type=F32)
    ones_row = lax.broadcasted_iota(jnp.int32, vt.shape, 0) % VT_ROWS == V_HEAD
    vt_ref[...] = jnp.where(ones_row, 1.0, vt).astype(BF16)
    kr_lanes = lax.broadcasted_iota(jnp.int32, (1, LANES), 1) < QK_ROPE
    kr = jnp.where(kr_lanes, kr_ref[...].astype(F32), 0.0)
    ss_r = jnp.sum(kr * kr, axis=-1, keepdims=True)
    k_rot = _rope(kr * gkr_ref[...], tables)
    g_qn = gqh_ref[:, :QK_NOPE]
    g_qr = gqh_ref[:, QK_NOPE:]
    g_kn = gkn_ref[...]
    scale = QK_HEAD ** -0.5 * LOG2E

    qa = [q[:, h * QK_PAD:h * QK_PAD + QK_NOPE] for h in heads]
    qr = [q[:, h * QK_PAD + QK_NOPE:(h + 1) * QK_PAD] for h in heads]
    kn = [k_nope[:, h * QK_NOPE:(h + 1) * QK_NOPE] for h in heads]
    first_copy = (lax.broadcasted_iota(jnp.int32, (1, LANES), 1) < QK_ROPE).astype(F32)
    qm = [r * first_copy for r in qr]
    ss_q = [jnp.sum(a * a + r * r, axis=-1, keepdims=True) for a, r in zip(qa, qm)]
    ss_k = [jnp.sum(x * x, axis=-1, keepdims=True) + ss_r for x in kn]
    inv_q = [lax.rsqrt(s * (1.0 / QK_HEAD) + RMS_EPS) * scale for s in ss_q]
    inv_k = [lax.rsqrt(s * (1.0 / QK_HEAD) + RMS_EPS) for s in ss_k]
    q_rot = [_rope_dup(r * g_qr, tables) for r in qr]
    for h in heads:
        q_ref[:, h * QK_PAD:h * QK_PAD + QK_NOPE] = (qa[h] * inv_q[h] * g_qn).astype(BF16)
        q_ref[:, h * QK_PAD + QK_NOPE:(h + 1) * QK_PAD] = (q_rot[h] * inv_q[h]).astype(BF16)
        k_ref[:, h * QK_PAD:h * QK_PAD + QK_NOPE] = (kn[h] * inv_k[h] * g_kn).astype(BF16)
        k_ref[:, h * QK_PAD + QK_NOPE:(h + 1) * QK_PAD] = (k_rot * inv_k[h]).astype(BF16)


def _mla_prep(proj, pos2, invf, g_q_lat, g_kv_lat, w_q, w_kn, w_vt, g_qh, g_kn, g_kr, batch, seq, tm=512):
    n = proj.shape[0]
    width = MLA_HEADS * QK_PAD
    ns = seq // tm
    tok = lambda col: (lambda b, i: (b * ns + i, col))
    const = lambda b, i: (0, 0)
    return pl.pallas_call(
        _mla_prep_kernel,
        grid=(batch, ns),
        in_specs=[
            pl.BlockSpec((tm, Q_LORA), tok(OFF_CQ // Q_LORA)),
            pl.BlockSpec((tm, KV_LORA), tok(OFF_CKV // KV_LORA)),
            pl.BlockSpec((tm, LANES), tok(OFF_KROPE // LANES)),
            pl.BlockSpec((tm, 1), tok(0)),
            pl.BlockSpec((1, LANES), const),
            pl.BlockSpec((1, Q_LORA), const),
            pl.BlockSpec((1, KV_LORA), const),
            pl.BlockSpec((Q_LORA, width), const),
            pl.BlockSpec((KV_LORA, MLA_HEADS * QK_NOPE), const),
            pl.BlockSpec((MLA_HEADS * VT_ROWS, KV_LORA), const),
            pl.BlockSpec((1, QK_PAD), const),
            pl.BlockSpec((1, LANES), const),
            pl.BlockSpec((1, LANES), const),
        ],
        out_specs=[
            pl.BlockSpec((tm, width), tok(0)),
            pl.BlockSpec((tm, width), tok(0)),
            pl.BlockSpec((MLA_HEADS * VT_ROWS, tm), lambda b, i: (b, i)),
        ],
        out_shape=[
            jax.ShapeDtypeStruct((n, width), BF16),
            jax.ShapeDtypeStruct((n, width), BF16),
            jax.ShapeDtypeStruct((batch * MLA_HEADS * VT_ROWS, seq), BF16),
        ],
        compiler_params=_cparams(("parallel", "parallel")),
        name="mla_prep",
    )(proj, proj, proj, pos2, invf, g_q_lat, g_kv_lat, w_q, w_kn, w_vt, g_qh, g_kn, g_kr)


def _attn_kernel(q_ref, k_ref, vt_ref, z_ref, o_ref, *, seq, tq):
    nq = seq // tq
    hq = tq // 2

    def chunk_mask(n_keys, query_offset):
        key_chunk = lax.broadcasted_iota(jnp.int32, (n_keys, hq), 0) // CHUNK
        qry_chunk = (lax.broadcasted_iota(jnp.int32, (n_keys, hq), 1) + query_offset) // CHUNK
        return key_chunk <= qry_chunk

    diag_mask_x = chunk_mask(hq, 0)
    diag_mask_y = chunk_mask(tq, hq)

    def qk(q, k):
        return lax.dot_general(k, q, (((1,), (1,)), ((), ())), preferred_element_type=F32)

    def softmax_step(st, carry, mask):
        m, acc = carry
        if mask is not None:
            st = jnp.where(mask, st, -jnp.inf)
        m_new = jnp.maximum(m, jnp.max(st, axis=0, keepdims=True))
        alpha = jnp.exp2(m - m_new)
        p = jnp.exp2(st - m_new)
        return p.astype(BF16), alpha, (m_new, acc)

    def pv(vt, p, alpha, carry):
        m, acc = carry
        return m, alpha * acc + jnp.dot(vt, p, preferred_element_type=F32)

    def full_tile(off):
        half = (k_ref[off:off + tq, :], vt_ref[:, off:off + tq], None)
        return half, half

    def diag_tile(off):
        return ((k_ref[off:off + hq, :], vt_ref[:, off:off + hq], diag_mask_x),
                (k_ref[off:off + tq, :], vt_ref[:, off:off + tq], diag_mask_y))

    steps = [(i, t) for i in range(nq) for t in range(i + 1)]
    qs, carries, tiles, scores, probs = {}, {}, {}, {}, {}

    def emit_qk(s):
        i, t = steps[s]
        if t == 0:
            qs[i] = (q_ref[i * tq:i * tq + hq, :], q_ref[i * tq + hq:(i + 1) * tq, :])
            init = (jnp.full((1, hq), -1e30, F32), jnp.zeros((VT_ROWS, hq), F32))
            carries[i] = [init, init]
        tiles[s] = diag_tile(i * tq) if t == i else full_tile(t * tq)
        scores[s] = [qk(qs[i][x], tiles[s][x][0]) for x in range(2)]

    def emit_softmax(s):
        i, _ = steps[s]
        st = scores.pop(s)
        out = []
        for x in range(2):
            p, alpha, carries[i][x] = softmax_step(st[x], carries[i][x], tiles[s][x][2])
            out.append((p, alpha))
        probs[s] = out

    def emit_pv(s):
        i, t = steps[s]
        tile = tiles.pop(s)
        for x, (p, alpha) in enumerate(probs.pop(s)):
            carries[i][x] = pv(tile[x][1], p, alpha, carries[i][x])
        if t == i:
            for x, (_, acc) in enumerate(carries.pop(i)):
                rows = slice(i * tq + x * hq, i * tq + (x + 1) * hq)
                o = (acc[:V_HEAD] / acc[V_HEAD:V_HEAD + 1]).T
                o_ref[rows, :] = (o * _silu(z_ref[rows, :].astype(F32))).astype(o_ref.dtype)

    emit_qk(0)
    for s in range(len(steps)):
        if s + 1 < len(steps):
            emit_qk(s + 1)
        emit_softmax(s)
        if s > 0:
            emit_pv(s - 1)
    emit_pv(len(steps) - 1)


def _attention(q, k, vt, proj, batch, seq, tq=512):
    n = q.shape[0]
    return pl.pallas_call(
        functools.partial(_attn_kernel, seq=seq, tq=tq),
        grid=(batch, MLA_HEADS),
        in_specs=[
            pl.BlockSpec((seq, QK_PAD), lambda b, h: (b, h)),
            pl.BlockSpec((seq, QK_PAD), lambda b, h: (b, h)),
            pl.BlockSpec((VT_ROWS, seq), lambda b, h: (b * MLA_HEADS + h, 0)),
            pl.BlockSpec((seq, V_HEAD), lambda b, h: (b, OFF_Z_MLA // V_HEAD + h)),
        ],
        out_specs=pl.BlockSpec((seq, V_HEAD), lambda b, h: (b, h)),
        out_shape=jax.ShapeDtypeStruct((n, MLA_WIDTH), BF16),
        compiler_params=_cparams(("parallel", "parallel")),
        name="mla_attention",
    )(q, k, vt, proj)


def _gdn_prep_kernel(q_ref, k_ref, v_ref, qh_ref, kh_ref, vh_ref, cwq_ref, cwk_ref, cwv_ref, gb_ref,
                     u_ref, w_ref, qd_ref, ai_ref, kdt_ref, gl_ref, *, tt):
    first = pl.program_id(1) == 0
    blk = GDN_BLOCK
    nb = tt // blk
    pair = 2 * LANES

    gb = gb_ref[...]
    rmod = lax.broadcasted_iota(jnp.int32, (tt, LANES), 0) % blk
    gc = gb
    s = 1
    while s < blk:
        gc = gc + jnp.where(rmod >= s, pltpu.roll(gc, s, 0), 0.0)
        s *= 2
    gc_t = [gc[p * blk:(p + 1) * blk].T for p in range(nb)]

    halo_rows = 16
    win = blk + halo_rows
    srow = lax.broadcasted_iota(jnp.int32, ((CONV_K - 1) * blk, win), 0)
    scol = lax.broadcasted_iota(jnp.int32, ((CONV_K - 1) * blk, win), 1)
    shift_mat = (scol == srow % blk + halo_rows - 1 - srow // blk).astype(BF16)

    def shifted(x_ref, h_ref):
        halo = jnp.where(first, jnp.zeros_like(h_ref[...]), h_ref[...])
        xe = jnp.concatenate([halo, x_ref[...]], axis=0)
        return [jnp.dot(shift_mat, xe[p * blk:p * blk + win], preferred_element_type=F32) for p in range(nb)]

    def conv_silu(x_ref, cw_ref, sh):
        cw = cw_ref[...]
        out = []
        for p in range(nb):
            y = x_ref[p * blk:(p + 1) * blk, :].astype(F32) * cw[CONV_K - 1:CONV_K]
            for s in range(CONV_K - 1):
                y = y + sh[p][s * blk:(s + 1) * blk] * cw[CONV_K - 2 - s:CONV_K - 1 - s]
            out.append(_silu(y))
        return jnp.concatenate(out, axis=0)

    def l2norm_heads(x, scale):
        parts = []
        for hh in range(GDN_GROUP):
            xs = x[:, hh * LANES:(hh + 1) * LANES]
            parts.append(xs * (lax.rsqrt(jnp.sum(xs * xs, axis=-1, keepdims=True) + L2_EPS) * scale))
        return jnp.concatenate(parts, axis=1)

    def lane_bcast(x, lane0):
        return jnp.concatenate(
            [jnp.broadcast_to(x[:, lane0 + hh:lane0 + hh + 1], (tt, LANES)) for hh in range(GDN_GROUP)], axis=1)

    sh_q = shifted(q_ref, qh_ref)
    sh_k = shifted(k_ref, kh_ref)
    q = l2norm_heads(conv_silu(q_ref, cwq_ref, sh_q), GDN_DK ** -0.5)
    sh_v = shifted(v_ref, vh_ref)
    k = l2norm_heads(conv_silu(k_ref, cwk_ref, sh_k), 1.0)
    v = conv_silu(v_ref, cwv_ref, sh_v)
    beta_b = lane_bcast(gb, 0)
    gcol_b = lane_bcast(gc, 16)
    egc = jnp.exp(gcol_b)
    kb = k * beta_b
    vb = v * beta_b
    kbe = kb * egc
    qd_ref[...] = (q * egc).astype(BF16)
    glast_b = jnp.concatenate(
        [jnp.broadcast_to(gcol_b[(p + 1) * blk - 1:(p + 1) * blk, :], (blk, gcol_b.shape[1])) for p in range(nb)], axis=0)
    kd = k * jnp.exp(glast_b - gcol_b)
    for p in range(nb):
        r = slice(p * blk, (p + 1) * blk)
        gl_ref[p * 8:(p + 1) * 8, :] = jnp.exp(glast_b[p * blk:p * blk + 8, :])
        for hh in range(GDN_GROUP):
            c = slice(hh * LANES, (hh + 1) * LANES)
            kdt_ref[c, r] = kd[r, c].T.astype(BF16)

    ri = lax.broadcasted_iota(jnp.int32, (blk, pair), 0)
    ci = lax.broadcasted_iota(jnp.int32, (blk, pair), 1) % LANES
    causal = ri >= ci
    strict = ri > ci
    zero = jnp.zeros((blk, LANES), BF16)

    def bdiag(x):
        return jnp.concatenate([jnp.concatenate([x[:, :LANES], zero], axis=1),
                                jnp.concatenate([zero, x[:, LANES:]], axis=1)], axis=0)

    def mm(a, b):
        return jnp.dot(a, b, preferred_element_type=F32)

    items = [(p, j) for p in range(nb) for j in range(GDN_GROUP // 2)]
    rc = lambda p, j: (slice(p * blk, (p + 1) * blk), slice(j * pair, (j + 1) * pair))

    kq = []
    for p, j in items:
        r, c = rc(p, j)
        lhs = jnp.concatenate([kb[r, c].astype(BF16), q[r, c].astype(BF16)], axis=0)
        kq.append(lax.dot_general(lhs, bdiag(k[r, c].astype(BF16)), (((1,), (1,)), ((), ())),
                                  preferred_element_type=F32))
    e = []
    for (p, j), kq_i in zip(items, kq):
        r, c = rc(p, j)
        grow = jnp.concatenate([gc_t[p][16 + 2 * j:17 + 2 * j, :], gc_t[p][17 + 2 * j:18 + 2 * j, :]], axis=1)
        diff = gcol_b[r, c] - grow
        decay = jnp.where(causal, jnp.exp(jnp.where(causal, diff, 0.0)), 0.0)
        ai_ref[r, c] = (kq_i[blk:] * decay).astype(BF16)
        e.append(jnp.where(strict, -(kq_i[:blk] * decay), 0.0))
    eb = [x.astype(BF16) for x in e]
    pw = [mm(x, bdiag(x)) for x in eb]
    span = 2
    while span * 2 < blk:
        pwb = [x.astype(BF16) for x in pw]
        both = [mm(jnp.concatenate([e_i.astype(BF16), p_i], axis=0), bdiag(p_i)) for e_i, p_i in zip(e, pwb)]
        e = [e_i + p_i + b_i[:blk] for e_i, p_i, b_i in zip(e, pw, both)]
        pw = [b_i[blk:] for b_i in both]
        span *= 2
    last = [mm(e_i.astype(BF16), bdiag(p_i.astype(BF16))) for e_i, p_i in zip(e, pw)]
    e = [e_i + p_i + l_i for e_i, p_i, l_i in zip(e, pw, last)]
    for (p, j), e_i in zip(items, e):
        r, _ = rc(p, j)
        for s in range(2):
            c = slice((2 * j + s) * LANES, (2 * j + s + 1) * LANES)
            rhs = jnp.concatenate([vb[r, c].astype(BF16), kbe[r, c].astype(BF16)], axis=1)
            sol = mm(e_i[:, s * LANES:(s + 1) * LANES].astype(BF16), rhs)
            u_ref[r, c] = vb[r, c] + sol[:, :LANES]
            w_ref[r, c] = (kbe[r, c] + sol[:, LANES:]).astype(BF16)


def _gdn_prep(proj, conv_w, gb, batch, seq, tt=1024):
    n = proj.shape[0]
    gw = GDN_GROUP * LANES
    ns = seq // tt
    ngrp = GDN_HEADS // GDN_GROUP
    col = lambda base: (lambda b, i, g: (b * ns + i, base // gw + g))
    halo = lambda base: (lambda b, i, g: (jnp.maximum((b * ns + i) * (tt // 16) - 1, 0), base // gw + g))
    cwc = lambda base: (lambda b, i, g: (0, base // gw + g))
    tok = lambda b, i, g: (b * ns + i, g)
    return pl.pallas_call(
        functools.partial(_gdn_prep_kernel, tt=tt),
        grid=(batch, ns, ngrp),
        in_specs=[
            pl.BlockSpec((tt, gw), col(OFF_QKV)),
            pl.BlockSpec((tt, gw), col(OFF_QKV + GDN_KEY_WIDTH)),
            pl.BlockSpec((tt, gw), col(OFF_QKV + 2 * GDN_KEY_WIDTH)),
            pl.BlockSpec((16, gw), halo(OFF_QKV)),
            pl.BlockSpec((16, gw), halo(OFF_QKV + GDN_KEY_WIDTH)),
            pl.BlockSpec((16, gw), halo(OFF_QKV + 2 * GDN_KEY_WIDTH)),
            pl.BlockSpec((CONV_K, gw), cwc(0)),
            pl.BlockSpec((CONV_K, gw), cwc(GDN_KEY_WIDTH)),
            pl.BlockSpec((CONV_K, gw), cwc(2 * GDN_KEY_WIDTH)),
            pl.BlockSpec((tt, LANES), tok),
        ],
        out_specs=[
            pl.BlockSpec((tt, gw), tok),
            pl.BlockSpec((tt, gw), tok),
            pl.BlockSpec((tt, gw), tok),
            pl.BlockSpec((tt, gw), tok),
            pl.BlockSpec((gw, tt), lambda b, i, g: (b * ngrp + g, i)),
            pl.BlockSpec((tt // GDN_BLOCK * 8, gw), tok),
        ],
        out_shape=[
            jax.ShapeDtypeStruct((n, GDN_WIDTH), F32),
            jax.ShapeDtypeStruct((n, GDN_WIDTH), BF16),
            jax.ShapeDtypeStruct((n, GDN_WIDTH), BF16),
            jax.ShapeDtypeStruct((n, GDN_WIDTH), BF16),
            jax.ShapeDtypeStruct((batch * GDN_HEADS * GDN_DK, seq), BF16),
            jax.ShapeDtypeStruct((n // GDN_BLOCK * 8, GDN_WIDTH), F32),
        ],
        compiler_params=_cparams(("parallel", "parallel", "parallel")),
        name="gdn_prep",
    )(proj, proj, proj, proj, proj, proj, conv_w, conv_w, conv_w, gb)


def _gdn_scan_kernel(u_ref, w_ref, qd_ref, ai_ref, kdt_ref, gl_ref, z_ref, g_ref, o_ref, s_ref, *, ts, hg):
    @pl.when(pl.program_id(2) == 0)
    def _():
        s_ref[...] = jnp.zeros_like(s_ref)

    blk = GDN_BLOCK
    gain = g_ref[...]
    for p in range(ts // blk):
        r = slice(p * blk, (p + 1) * blk)
        for hh in range(hg):
            c = slice(hh * LANES, (hh + 1) * LANES)
            state = s_ref[hh]
            lhs1 = jnp.concatenate([w_ref[r, c], qd_ref[r, c]], axis=0)
            r1 = jnp.dot(lhs1, state.astype(BF16), preferred_element_type=F32)
            v_new = u_ref[r, c] - r1[:blk]
            lhs2 = jnp.concatenate([ai_ref[r, c], kdt_ref[c, r]], axis=0)
            r2 = jnp.dot(lhs2, v_new.astype(BF16), preferred_element_type=F32)
            o = r1[blk:] + r2[:blk]
            gl = jnp.concatenate([gl_ref[p * 8:(p + 1) * 8, c]] * (blk // 8), axis=0)
            s_ref[hh] = state * gl + r2[blk:]
            z = z_ref[r, c].astype(F32)
            o_ref[r, c] = (_rms_rows(o, gain) * _silu(z)).astype(o_ref.dtype)


def _gdn_scan(u, w, qd, ai, kdt, gl8, proj, g_out, batch, seq, ts=1024, hg=8):
    n = u.shape[0]
    gw = hg * LANES
    ns = seq // ts
    ngrp = GDN_HEADS // hg
    tok = lambda b, g, i: (b * ns + i, g)
    return pl.pallas_call(
        functools.partial(_gdn_scan_kernel, ts=ts, hg=hg),
        grid=(batch, ngrp, ns),
        in_specs=[
            pl.BlockSpec((ts, gw), tok),
            pl.BlockSpec((ts, gw), tok),
            pl.BlockSpec((ts, gw), tok),
            pl.BlockSpec((ts, gw), tok),
            pl.BlockSpec((gw, ts), lambda b, g, i: (b * ngrp + g, i)),
            pl.BlockSpec((ts // GDN_BLOCK * 8, gw), tok),
            pl.BlockSpec((ts, gw), lambda b, g, i: (b * ns + i, OFF_Z_GDN // gw + g)),
            pl.BlockSpec((1, LANES), lambda b, g, i: (0, 0)),
        ],
        out_specs=pl.BlockSpec((ts, gw), tok),
        out_shape=jax.ShapeDtypeStruct((n, GDN_WIDTH), BF16),
        scratch_shapes=[pltpu.VMEM((hg, GDN_DK, GDN_DV), F32)],
        compiler_params=_cparams(("parallel", "parallel", "arbitrary")),
        name="gdn_scan",
    )(u, w, qd, ai, kdt, gl8, proj, g_out)


def _merge_kernel(a1_ref, a2_ref, w1_ref, w2_ref, g1_ref, g2_ref, b_ref, h_ref):
    y1 = jnp.dot(a1_ref[...], w1_ref[...], preferred_element_type=F32)
    y2 = jnp.dot(a2_ref[...], w2_ref[...], preferred_element_type=F32)
    s1 = _sigmoid(g1_ref[...].astype(F32) + b_ref[0:1, :])
    s2 = _sigmoid(g2_ref[...].astype(F32) + b_ref[1:2, :])
    h_ref[...] = (s1 * y1 + s2 * y2).astype(h_ref.dtype)


def _merge(o_mla, o_gdn, w_o_mla, w_o_gdn, proj, b_gate, tm=1024, tn=1024):
    n = o_mla.shape[0]
    return pl.pallas_call(
        _merge_kernel,
        grid=(n // tm, D_MODEL // tn),
        in_specs=[
            pl.BlockSpec((tm, MLA_WIDTH), lambda i, j: (i, 0)),
            pl.BlockSpec((tm, GDN_WIDTH), lambda i, j: (i, 0)),
            pl.BlockSpec((MLA_WIDTH, tn), lambda i, j: (0, j)),
            pl.BlockSpec((GDN_WIDTH, tn), lambda i, j: (0, j)),
            pl.BlockSpec((tm, tn), lambda i, j: (i, OFF_GATE_MLA // tn + j)),
            pl.BlockSpec((tm, tn), lambda i, j: (i, OFF_GATE_GDN // tn + j)),
            pl.BlockSpec((2, tn), lambda i, j: (0, j)),
        ],
        out_specs=pl.BlockSpec((tm, tn), lambda i, j: (i, j)),
        out_shape=jax.ShapeDtypeStruct((n, D_MODEL), BF16),
        compiler_params=_cparams(("parallel", "parallel")),
        name="merge",
    )(o_mla, o_gdn, w_o_mla, w_o_gdn, proj, proj, b_gate)


def _out_kernel(h_ref, w_ref, x_ref, o_ref):
    o_ref[...] = x_ref[...] + jnp.dot(h_ref[...], w_ref[...], preferred_element_type=F32)


def _out_proj(h, w_out, x2, tm=1024, tn=1024):
    n = h.shape[0]
    return pl.pallas_call(
        _out_kernel,
        grid=(n // tm, D_MODEL // tn),
        in_specs=[
            pl.BlockSpec((tm, D_MODEL), lambda i, j: (i, 0)),
            pl.BlockSpec((D_MODEL, tn), lambda i, j: (0, j)),
            pl.BlockSpec((tm, tn), lambda i, j: (i, j)),
        ],
        out_specs=pl.BlockSpec((tm, tn), lambda i, j: (i, j)),
        out_shape=jax.ShapeDtypeStruct((n, D_MODEL), F32),
        compiler_params=_cparams(("parallel", "parallel")),
        name="out_proj",
    )(h, w_out, x2)


def _pad_cols(a, width):
    return jnp.pad(a, ((0, 0), (0, width - a.shape[1])))


def kernel(x, positions, w_norm, w_in, b_gate, w_q_up, w_kv_up, g_q_latent, g_kv_latent, g_q_head, g_k_head,
           w_o_mla, conv_w, a_log, dt_bias, g_gdn_out, w_o_gdn, w_out):
    batch, seq, _ = x.shape
    n = batch * seq
    x2 = x.reshape(n, D_MODEL)

    o_cq, o_ckv, o_kr = 0, Q_LORA, Q_LORA + KV_LORA
    o_zm = o_kr + QK_ROPE
    o_qkv = o_zm + MLA_WIDTH
    o_b = o_qkv + GDN_QKV
    o_a = o_b + GDN_HEADS
    o_zg = o_a + GDN_HEADS
    o_gm = o_zg + GDN_WIDTH
    o_gg = o_gm + D_MODEL
    w_t = w_in.T
    rows = lambda lo, count: w_t[lo:lo + count]
    w_t_bf = w_t.astype(BF16)
    tile = 1024
    segments = ((o_gm, D_MODEL), (o_gg, D_MODEL), (o_qkv, GDN_QKV), (o_zm, MLA_WIDTH), (o_zg, GDN_WIDTH),
                (o_cq, Q_LORA), (o_ckv, tile))
    src_rows = tuple(lo + t * tile for lo, width in segments for t in range(width // tile))
    assert len(src_rows) * tile == PROJ_WIDTH

    ngrp = GDN_HEADS // GDN_GROUP

    def gate_rows(beta_part, decay_part):
        width = beta_part.shape[1]
        out = []
        for g in range(ngrp):
            hs = slice(g * GDN_GROUP, (g + 1) * GDN_GROUP)
            out += [beta_part[hs], jnp.zeros((16 - GDN_GROUP, width), F32),
                    decay_part[hs], jnp.zeros((LANES - 16 - GDN_GROUP, width), F32)]
        return jnp.concatenate(out, axis=0)

    w_bd_t = gate_rows(rows(o_b, GDN_HEADS), rows(o_a, GDN_HEADS)).astype(BF16)
    zeros_h = jnp.zeros((GDN_HEADS, 1), F32)
    par = jnp.concatenate([gate_rows(zeros_h, a_log[:, None]).T, gate_rows(zeros_h, dt_bias[:, None]).T,
                           jnp.zeros((6, ngrp * LANES), F32)], axis=0)

    w_q3 = w_q_up.reshape(Q_LORA, MLA_HEADS, QK_HEAD)
    w_q = jnp.concatenate([w_q3, w_q3[:, :, QK_NOPE:]], axis=2)
    w_q = w_q.reshape(Q_LORA, MLA_HEADS * QK_PAD).astype(BF16)
    w_kv3 = w_kv_up.reshape(KV_LORA, MLA_HEADS, QK_NOPE + V_HEAD)
    w_kn = w_kv3[:, :, :QK_NOPE].reshape(KV_LORA, MLA_HEADS * QK_NOPE).astype(BF16)
    w_vt = jnp.pad(w_kv3[:, :, QK_NOPE:], ((0, 0), (0, 0), (0, VT_ROWS - V_HEAD)))
    w_vt = w_vt.reshape(KV_LORA, MLA_HEADS * VT_ROWS).T.astype(BF16)
    g_qh = jnp.concatenate([g_q_head, g_q_head[QK_NOPE:]])[None, :]
    g_kn = g_k_head[None, :QK_NOPE]
    g_kr = _pad_cols(g_k_head[None, QK_NOPE:], LANES)
    inv_freq = ROPE_THETA ** (-jnp.arange(0, QK_ROPE, 2, dtype=F32) / QK_ROPE)
    invf = jnp.tile(inv_freq, LANES // (QK_ROPE // 2))[None, :]
    pos2 = positions.reshape(n, 1)

    xn, gb = _norm_gate(x2, w_norm[None, :], w_bd_t, par)
    proj = _matmul_nt(xn, w_t_bf, BF16, 1024, tile, "in_proj", src_rows)

    q, k, vt = _mla_prep(proj, pos2, invf, g_q_latent[None, :], g_kv_latent[None, :], w_q, w_kn, w_vt, g_qh, g_kn,
                         g_kr, batch, seq)
    o_mla = _attention(q, k, vt, proj, batch, seq)

    u, w, qd, ai, kdt, gl8 = _gdn_prep(proj, conv_w, gb, batch, seq)
    o_gdn = _gdn_scan(u, w, qd, ai, kdt, gl8, proj, g_gdn_out[None, :], batch, seq)

    h = _merge(o_mla, o_gdn, w_o_mla.astype(BF16), w_o_gdn.astype(BF16), proj, b_gate)
    out = _out_proj(h, w_out.astype(BF16), x2)
    return out.reshape(batch, seq, D_MODEL)
```
